```python
import math
import jax, jax.numpy as jnp
from jax import lax
import numpy as np

D_MODEL = 2048
BATCH = 2
SEQ = 16384
DEPTH = 1

N_META = 16
CHUNK = 128
PAD = CHUNK - N_META
A_HEADS = 8
A_DK = 64
A_DV = 2 * A_DK
Q_W = A_HEADS * 2 * A_DK
V_W = A_HEADS * A_DV
D_SSM = D_MODEL
S_HEADDIM = 64
S_HEADS = D_SSM // S_HEADDIM
S_GROUPS = 4
S_HPG = S_HEADS // S_GROUPS
S_STATE = 128
S_CONV = 4
CONV_DIM = D_SSM + 2 * S_GROUPS * S_STATE
SPLIT_SIZES = [Q_W, Q_W, V_W, V_W, D_SSM, CONV_DIM, S_HEADS, D_MODEL, D_MODEL]
P_IN = sum(SPLIT_SIZES)
SPLIT_IDX = [int(v) for v in np.cumsum(SPLIT_SIZES)[:-1]]
EPS = 1e-6
NEG = -1e30

kernel_name = 'hybrid_diffattn_ssd_gated_merge'


def rmsnorm(x, g):
    xf = x.astype(jnp.float32)
    y = xf * lax.rsqrt(jnp.mean(xf * xf, axis=-1, keepdims=True) + EPS)
    return (y * g.astype(jnp.float32)).astype(x.dtype)


def pad_front(a, n):
    cfg = [(0, 0)] * a.ndim
    cfg[1] = (n, 0)
    return jnp.pad(a, cfg)


def diff_attention(q, k, v, lam):
    b, lp = q.shape[0], q.shape[1]
    nblk = lp // CHUNK
    slopes = 2.0 ** (-8.0 * jnp.arange(1, A_HEADS + 1, dtype=jnp.float32) / A_HEADS)
    kpos = jnp.arange(lp)
    kf = k.astype(jnp.float32)
    vf = v.astype(jnp.float32)
    scale = A_DK ** -0.5

    def block(i):
        q0 = i * CHUNK
        qb = lax.dynamic_slice_in_dim(q, q0, CHUNK, axis=1).astype(jnp.float32) * scale
        s = jnp.einsum('bqhmd,bkhmd->bhmqk', qb, kf)
        qpos = q0 + jnp.arange(CHUNK)
        dist = jnp.abs(qpos[:, None] - kpos[None, :]).astype(jnp.float32)
        s = s - (slopes[:, None, None] * dist)[None, :, None]
        valid = (kpos[None, :] <= qpos[:, None]) & (kpos[None, :] >= PAD)
        s = jnp.where(valid, s, NEG)
        p = jax.nn.softmax(s, axis=-1)
        a = p[:, :, 0] - lam * p[:, :, 1]
        return jnp.einsum('bhqk,bkhe->bqhe', a, vf)

    o = lax.map(block, jnp.arange(nblk))
    return o.transpose(1, 0, 2, 3, 4).reshape(b, lp, A_HEADS, A_DV)


def ssd_chunked(xs, dt, a, bm, cm):
    b, lp = xs.shape[0], xs.shape[1]
    c = lp // CHUNK
    xdt = (xs * dt[..., None]).reshape(b, c, CHUNK, S_GROUPS, S_HPG, S_HEADDIM)
    da = (dt.reshape(b, c, CHUNK, S_GROUPS, S_HPG) * a.reshape(S_GROUPS, S_HPG)).transpose(0, 1, 3, 4, 2)
    acs = jnp.cumsum(da, axis=-1)
    bb = bm.reshape(b, c, CHUNK, S_GROUPS, S_STATE)
    cc = cm.reshape(b, c, CHUNK, S_GROUPS, S_STATE)
    tril = jnp.tril(jnp.ones((CHUNK, CHUNK), dtype=bool))
    lmat = jnp.exp(jnp.where(tril, acs[..., :, None] - acs[..., None, :], -jnp.inf))
    cb = jnp.einsum('bclgn,bcsgn->bcgls', cc, bb)
    y_diag = jnp.einsum('bcgls,bcghls,bcsghp->bclghp', cb, lmat, xdt)
    decay_states = jnp.exp(acs[..., -1:] - acs)
    states = jnp.einsum('bclgn,bcghl,bclghp->bcghpn', bb, decay_states, xdt)
    chunk_decay = jnp.exp(acs[..., -1])

    def step(s, inp):
        st, dec = inp
        return dec[..., None, None] * s + st, s

    init = jnp.zeros((b, S_GROUPS, S_HPG, S_HEADDIM, S_STATE), jnp.float32)
    _, prev = lax.scan(step, init, (states.transpose(1, 0, 2, 3, 4, 5), chunk_decay.transpose(1, 0, 2, 3)))
    prev = prev.transpose(1, 0, 2, 3, 4, 5)
    y_off = jnp.einsum('bclgn,bcghpn,bcghl->bclghp', cc, prev, jnp.exp(acs))
    return (y_diag + y_off).reshape(b, lp, S_HEADS, S_HEADDIM)


def setup_inputs(seed: int = 0) -> dict:
    key = jax.random.key(seed)
    ks = jax.random.split(key, 20)
    f32 = jnp.float32
    nrm = lambda k, s: jax.random.normal(k, s, f32)
    x = nrm(ks[0], (BATCH, SEQ, D_MODEL))
    meta = nrm(ks[1], (N_META, D_MODEL))
    norm_g = 1.0 + 0.02 * nrm(ks[2], (DEPTH, D_MODEL))
    w_in = nrm(ks[3], (DEPTH, D_MODEL, P_IN)) * D_MODEL ** -0.5
    conv_w = nrm(ks[4], (DEPTH, S_CONV, CONV_DIM)) * S_CONV ** -0.5
    conv_b = 0.02 * nrm(ks[5], (DEPTH, CONV_DIM))
    u = jax.random.uniform(ks[6], (DEPTH, S_HEADS), f32)
    dt0 = jnp.exp(u * (math.log(0.1) - math.log(0.001)) + math.log(0.001))
    dt_bias = dt0 + jnp.log(-jnp.expm1(-dt0))
    a_log = jnp.log(jax.random.uniform(ks[7], (DEPTH, S_HEADS), f32, minval=1.0, maxval=16.0))
    d_skip = 1.0 + 0.02 * nrm(ks[8], (DEPTH, S_HEADS))
    ssm_norm_g = 1.0 + 0.02 * nrm(ks[9], (DEPTH, D_SSM))
    lam_q1 = 0.1 * nrm(ks[10], (DEPTH, A_DK))
    lam_k1 = 0.1 * nrm(ks[11], (DEPTH, A_DK))
    lam_q2 = 0.1 * nrm(ks[12], (DEPTH, A_DK))
    lam_k2 = 0.1 * nrm(ks[13], (DEPTH, A_DK))
    subln_g = 1.0 + 0.02 * nrm(ks[14], (DEPTH, A_DV))
    w_br_attn = nrm(ks[15], (DEPTH, V_W, D_MODEL)) * V_W ** -0.5
    w_br_ssm = nrm(ks[16], (DEPTH, D_SSM, D_MODEL)) * D_SSM ** -0.5
    w_out = nrm(ks[17], (DEPTH, D_MODEL, D_MODEL)) * D_MODEL ** -0.5
    final_g = 1.0 + 0.02 * nrm(ks[18], (D_MODEL,))
    return {'x': x, 'meta': meta, 'norm_g': norm_g, 'w_in': w_in, 'conv_w': conv_w,
            'conv_b': conv_b, 'dt_bias': dt_bias, 'a_log': a_log, 'd_skip': d_skip,
            'ssm_norm_g': ssm_norm_g, 'lam_q1': lam_q1, 'lam_k1': lam_k1,
            'lam_q2': lam_q2, 'lam_k2': lam_k2, 'subln_g': subln_g,
            'w_br_attn': w_br_attn, 'w_br_ssm': w_br_ssm, 'w_out': w_out, 'final_g': final_g}


def reference(x, meta, norm_g, w_in, conv_w, conv_b, dt_bias, a_log, d_skip, ssm_norm_g,
              lam_q1, lam_k1, lam_q2, lam_k2, subln_g, w_br_attn, w_br_ssm, w_out, final_g):
    b, s, d = x.shape
    h = jnp.concatenate([jnp.broadcast_to(meta[None].astype(x.dtype), (b, N_META, d)), x], axis=1)
    L = s + N_META
    for l in range(DEPTH):
        u = rmsnorm(h, norm_g[l])
        zp = u @ w_in[l]
        q, k, v, g_att, z_ssm, xbc, dt, m_att, m_ssm = jnp.split(zp, SPLIT_IDX, axis=-1)

        lam_init = 0.8 - 0.6 * math.exp(-0.3 * l)
        lq1, lk1 = lam_q1[l].astype(jnp.float32), lam_k1[l].astype(jnp.float32)
        lq2, lk2 = lam_q2[l].astype(jnp.float32), lam_k2[l].astype(jnp.float32)
        lam = jnp.exp(jnp.sum(lq1 * lk1)) - jnp.exp(jnp.sum(lq2 * lk2)) + lam_init
        qa = pad_front(q.reshape(b, L, A_HEADS, 2, A_DK), PAD)
        ka = pad_front(k.reshape(b, L, A_HEADS, 2, A_DK), PAD)
        va = pad_front(v.reshape(b, L, A_HEADS, A_DV), PAD)
        oa = diff_attention(qa, ka, va, lam)[:, PAD:]
        oa = rmsnorm(oa, subln_g[l]) * (1.0 - lam_init)
        oa = oa.reshape(b, L, V_W).astype(x.dtype) * jax.nn.silu(g_att)
        y_att = oa @ w_br_attn[l]

        xbc = lax.conv_general_dilated(xbc, conv_w[l][:, None, :].astype(xbc.dtype), (1,),
                                       [(S_CONV - 1, 0)], dimension_numbers=('NWC', 'WIO', 'NWC'),
                                       feature_group_count=CONV_DIM)
        xbc = jax.nn.silu(xbc + conv_b[l].astype(xbc.dtype))
        xs, bm, cm = jnp.split(xbc.astype(jnp.float32), [D_SSM, D_SSM + S_GROUPS * S_STATE], axis=-1)
        xs = xs.reshape(b, L, S_HEADS, S_HEADDIM)
        bm = bm.reshape(b, L, S_GROUPS, S_STATE)
        cm = cm.reshape(b, L, S_GROUPS, S_STATE)
        dtf = jax.nn.softplus(dt.astype(jnp.float32) + dt_bias[l].astype(jnp.float32))
        a = -jnp.exp(a_log[l].astype(jnp.float32))
        ys = ssd_chunked(pad_front(xs, PAD), pad_front(dtf, PAD), a,
                         pad_front(bm, PAD), pad_front(cm, PAD))[:, PAD:]
        ys = ys + xs * d_skip[l].astype(jnp.float32)[None, None, :, None]
        yg = ys.reshape(b, L, D_SSM) * jax.nn.silu(z_ssm.astype(jnp.float32))
        yg = rmsnorm(yg.reshape(b, L, S_GROUPS, D_SSM // S_GROUPS),
                     ssm_norm_g[l].reshape(S_GROUPS, D_SSM // S_GROUPS))
        y_ssm = yg.reshape(b, L, D_SSM).astype(x.dtype) @ w_br_ssm[l]

        merged = jax.nn.sigmoid(m_att) * y_att + jax.nn.sigmoid(m_ssm) * y_ssm
        h = h + merged @ w_out[l]
    return rmsnorm(h[:, N_META:], final_g)
```

```python
import functools
import math

import jax
import jax.numpy as jnp
from jax import lax
from jax.experimental import pallas as pl
from jax.experimental.pallas import tpu as pltpu

F32 = jnp.float32
BF16 = jnp.bfloat16

D_MODEL = 2048
N_META = 16
PREFIX = 128
N_PAD = PREFIX - N_META
A_HEADS = 8
A_DK = 64
A_DV = 128
Q_W = A_HEADS * 2 * A_DK
V_W = A_HEADS * A_DV
D_SSM = 2048
S_HEADDIM = 64
S_HEADS = D_SSM // S_HEADDIM
S_GROUPS = 4
S_STATE = 128
S_CONV = 4
GROUP_W = D_SSM // S_GROUPS
BC_W = S_GROUPS * S_STATE
CONV_DIM = D_SSM + 2 * BC_W
DT_W = 512
EPS = 1e-6
NEG = -1e30
LAM_INIT = 0.8 - 0.6 * math.exp(-0.3 * 0)

OFF_Z = 0
OFF_MATT = OFF_Z + D_SSM
OFF_MSSM = OFF_MATT + D_MODEL
OFF_XBC = OFF_MSSM + D_MODEL
OFF_G = OFF_XBC + CONV_DIM
OFF_DT = OFF_G + V_W
REST_W = OFF_DT + DT_W

VMEM_LIMIT = 56 * 1024 * 1024


def _cparams(sem):
    return pltpu.CompilerParams(dimension_semantics=sem, vmem_limit_bytes=VMEM_LIMIT)


def _inproj_kernel(x_ref, g_ref, w_ref, o_ref, u_ref):
    @pl.when(pl.program_id(1) == 0)
    def _():
        x = x_ref[...]
        ms = jnp.mean(x * x, axis=-1, keepdims=True)
        u_ref[...] = (x * lax.rsqrt(ms + EPS) * g_ref[...]).astype(BF16)

    o_ref[...] = jnp.dot(u_ref[...], w_ref[...], preferred_element_type=F32).astype(o_ref.dtype)


def _inproj(x, g, w, out_dtype, tm, tn):
    m, d = x.shape
    n = w.shape[1]
    tm = min(tm, m)
    return pl.pallas_call(
        _inproj_kernel,
        grid=(m // tm, n // tn),
        in_specs=[
            pl.BlockSpec((tm, d), lambda i, j: (i, 0)),
            pl.BlockSpec((1, d), lambda i, j: (0, 0)),
            pl.BlockSpec((d, tn), lambda i, j: (0, j)),
        ],
        out_specs=pl.BlockSpec((tm, tn), lambda i, j: (i, j)),
        out_shape=jax.ShapeDtypeStruct((m, n), out_dtype),
        scratch_shapes=[pltpu.VMEM((tm, d), BF16)],
        compiler_params=_cparams(("parallel", "arbitrary")),
        name="in_proj",
    )(x, g, w)


def _attn_kernel(slope_ref, lq1_ref, lk1_ref, lq2_ref, lk2_ref, sg_ref,
                 q_ref, k_ref, v_ref, km_ref, vm_ref, g_ref, o_ref,
                 m_ref, l_ref, acc_ref, *, tq):
    h = pl.program_id(1)
    qi = pl.program_id(2)
    slope = slope_ref[h]

    q = q_ref[...]
    lane = lax.broadcasted_iota(jnp.int32, q.shape, 1)
    zero = jnp.zeros_like(q)
    qs = jnp.concatenate([jnp.where(lane < A_DK, q, zero),
                          jnp.where(lane >= A_DK, q, zero)], axis=0)

    def step(k, v, bias, mask):
        s = lax.dot_general(qs, k, (((1,), (1,)), ((), ())), preferred_element_type=F32)
        s = s + jnp.concatenate([bias, bias], axis=0)
        if mask is not None:
            s = jnp.where(jnp.concatenate([mask, mask], axis=0), s, NEG)
        m_old = m_ref[...]
        m_new = jnp.maximum(m_old, jnp.max(s, axis=-1, keepdims=True))
        p = jnp.exp(s - m_new)
        alpha = jnp.exp(m_old - m_new)
        l_ref[...] = alpha * l_ref[...] + jnp.sum(p, axis=-1, keepdims=True)
        acc_ref[...] = alpha * acc_ref[...] + jnp.dot(p.astype(BF16), v, preferred_element_type=F32)
        m_ref[...] = m_new

    m_ref[...] = jnp.full(m_ref.shape, NEG, F32)
    l_ref[...] = jnp.zeros(l_ref.shape, F32)
    acc_ref[...] = jnp.zeros(acc_ref.shape, F32)

    r_p = lax.broadcasted_iota(jnp.int32, (tq, PREFIX), 0)
    c_p = lax.broadcasted_iota(jnp.int32, (tq, PREFIX), 1)
    dist_p = (qi * tq + PREFIX + r_p - c_p).astype(F32)
    step(km_ref[...], vm_ref[...], -slope * dist_p, c_p >= N_PAD)

    r_i = lax.broadcasted_iota(jnp.int32, (tq, tq), 0)
    c_i = lax.broadcasted_iota(jnp.int32, (tq, tq), 1)
    rel = (r_i - c_i).astype(F32)

    def body(kc, carry):
        off = pl.multiple_of(kc * tq, tq)
        k = k_ref[pl.ds(off, tq), :]
        v = v_ref[pl.ds(off, tq), :]
        blk = ((qi - kc) * tq).astype(F32)
        step(k, v, -slope * (rel + blk), None)
        return carry

    lax.fori_loop(0, qi, body, 0)

    off = pl.multiple_of(qi * tq, tq)
    step(k_ref[pl.ds(off, tq), :], v_ref[pl.ds(off, tq), :], -slope * rel, c_i <= r_i)

    lam = (jnp.exp(jnp.sum(lq1_ref[...] * lk1_ref[...], axis=-1, keepdims=True))
           - jnp.exp(jnp.sum(lq2_ref[...] * lk2_ref[...], axis=-1, keepdims=True)) + LAM_INIT)
    o = acc_ref[...] / l_ref[...]
    oa = o[:tq] - lam * o[tq:]
    ms = jnp.mean(oa * oa, axis=-1, keepdims=True)
    oa = oa * lax.rsqrt(ms + EPS) * sg_ref[...] * (1.0 - LAM_INIT)
    g = g_ref[...]
    o_ref[...] = (oa * (g * jax.nn.sigmoid(g))).astype(o_ref.dtype)


def _attention(slopes, lq1, lk1, lq2, lk2, subln_g, qkv, kv0, rest, tq):
    b, s, _ = qkv.shape
    tq = min(tq, s)
    nh = A_HEADS
    small = lambda w: pl.BlockSpec((1, w), lambda bi, hi, qi: (0, 0))
    return pl.pallas_call(
        functools.partial(_attn_kernel, tq=tq),
        grid=(b, nh, s // tq),
        in_specs=[
            pl.BlockSpec(memory_space=pltpu.SMEM),
            small(A_DK), small(A_DK), small(A_DK), small(A_DK), small(A_DV),
            pl.BlockSpec((None, tq, 128), lambda bi, hi, qi: (bi, qi, hi)),
            pl.BlockSpec((None, s, 128), lambda bi, hi, qi: (bi, 0, nh + hi)),
            pl.BlockSpec((None, s, 128), lambda bi, hi, qi: (bi, 0, 2 * nh + hi)),
            pl.BlockSpec((PREFIX, 128), lambda bi, hi, qi: (0, hi)),
            pl.BlockSpec((PREFIX, 128), lambda bi, hi, qi: (0, nh + hi)),
            pl.BlockSpec((None, tq, 128), lambda bi, hi, qi: (bi, qi, OFF_G // 128 + hi)),
        ],
        out_specs=pl.BlockSpec((None, tq, 128), lambda bi, hi, qi: (bi, qi, hi)),
        out_shape=jax.ShapeDtypeStruct((b, s, V_W), BF16),
        scratch_shapes=[pltpu.VMEM((2 * tq, 1), F32), pltpu.VMEM((2 * tq, 1), F32),
                        pltpu.VMEM((2 * tq, A_DV), F32)],
        compiler_params=_cparams(("parallel", "parallel", "arbitrary")),
        name="diff_attention",
    )(slopes, lq1, lk1, lq2, lk2, subln_g, qkv, qkv, qkv, kv0, kv0, rest)


def _silu(x):
    return x * jax.nn.sigmoid(x)


def _ssd_kernel(z_ref, xbc_ref, dt_ref, cw_ref, cb_ref, dtb_ref, alog_ref, dskip_ref, ng_ref,
                exp_ref, st0_ref, tail0_ref,
                y_ref, stout_ref, tailout_ref,
                hist_ref, state_ref, *, q, n_pad):
    c = pl.program_id(1)

    @pl.when(c == 0)
    def _():
        hist_ref[0:8, :] = tail0_ref[...]
        state_ref[...] = st0_ref[...]

    hist_ref[8:8 + q, :] = xbc_ref[...]
    conv = cb_ref[...]
    for j in range(S_CONV):
        conv = conv + cw_ref[j:j + 1, :] * hist_ref[8 - (S_CONV - 1) + j:8 - (S_CONV - 1) + j + q, :]
    hist_ref[0:8, :] = hist_ref[q:q + 8, :]
    xbc = _silu(conv)

    dt_raw = dt_ref[:, :S_HEADS] + dtb_ref[...]
    dt = jnp.maximum(dt_raw, 0.0) + jnp.log(1.0 + jnp.exp(-jnp.abs(dt_raw)))
    if n_pad:
        row = lax.broadcasted_iota(jnp.int32, (q, 1), 0)
        valid = row >= n_pad
        xbc = jnp.where(valid, xbc, 0.0)
        dt = jnp.where(valid, dt, 0.0)
    a = -jnp.exp(alog_ref[...])
    da = dt * a

    r_i = lax.broadcasted_iota(jnp.int32, (q, q), 0)
    c_i = lax.broadcasted_iota(jnp.int32, (q, q), 1)
    tril = c_i <= r_i
    acs = jnp.dot(tril.astype(F32), da, preferred_element_type=F32, precision=lax.Precision.HIGHEST)
    acs_t = acs.T

    wide = jnp.dot(jnp.concatenate([dt, acs], axis=0), exp_ref[...],
                   preferred_element_type=F32, precision=lax.Precision.HIGHEST)
    dt_w = wide[:q]
    acs_w = wide[q:]
    acs_last = acs_w[q - 1:q, :]

    xs = xbc[:, :D_SSM]
    bm = xbc[:, D_SSM:D_SSM + BC_W].astype(BF16)
    cm = xbc[:, D_SSM + BC_W:].astype(BF16)
    xdt = xs * dt_w
    xdt_b = xdt.astype(BF16)
    xdec_b = (xdt * jnp.exp(acs_last - acs_w)).astype(BF16)
    state_old = state_ref[...]
    state_b = state_old.astype(BF16)
    lane = lax.broadcasted_iota(jnp.int32, (q, 128), 1)

    y_groups = []
    st_parts = []
    hpg = S_HEADS // S_GROUPS
    for g in range(S_GROUPS):
        bg = bm[:, g * S_STATE:(g + 1) * S_STATE]
        cg = cm[:, g * S_STATE:(g + 1) * S_STATE]
        cbm = lax.dot_general(cg, bg, (((1,), (1,)), ((), ())), preferred_element_type=F32)
        gs = slice(g * GROUP_W, (g + 1) * GROUP_W)
        y_off = jnp.dot(cg, state_b[:, gs], preferred_element_type=F32)
        st_parts.append(lax.dot_general(bg, xdec_b[:, gs], (((0,), (0,)), ((), ())),
                                        preferred_element_type=F32))
        pairs = []
        for pair in range(hpg // 2):
            h0 = g * hpg + 2 * pair
            cols = slice(h0 * S_HEADDIM, (h0 + 2) * S_HEADDIM)
            xp = xdt_b[:, cols]
            ys = []
            for hh in (h0, h0 + 1):
                seg = acs[:, hh:hh + 1] - acs_t[hh:hh + 1, :]
                lmat = jnp.exp(jnp.where(tril, seg, -jnp.inf))
                ys.append(jnp.dot((cbm * lmat).astype(BF16), xp, preferred_element_type=F32))
            pairs.append(jnp.where(lane < S_HEADDIM, ys[0], ys[1]))
        y_groups.append(jnp.concatenate(pairs, axis=1) + y_off * jnp.exp(acs_w[:, gs]))

    y = jnp.concatenate(y_groups, axis=1)
    state_new = jnp.exp(acs_last) * state_old + jnp.concatenate(st_parts, axis=1)
    state_ref[...] = state_new

    y = y + xs * dskip_ref[...]
    z = z_ref[...]
    yg = y * _silu(z)
    outs = []
    for g in range(S_GROUPS):
        blk = yg[:, g * GROUP_W:(g + 1) * GROUP_W]
        ms = jnp.mean(blk * blk, axis=-1, keepdims=True)
        outs.append(blk * lax.rsqrt(ms + EPS))
    y_ref[...] = (jnp.concatenate(outs, axis=1) * ng_ref[...]).astype(y_ref.dtype)

    @pl.when(c == pl.num_programs(1) - 1)
    def _():
        stout_ref[...] = state_new
        tailout_ref[...] = hist_ref[0:8, :]


def _ssd(rest, conv_w, conv_b, dt_bias, a_log, dskip_w, norm_g, expand, state0, tail0, q, n_pad):
    b, s, _ = rest.shape
    q = min(q, s)
    const = lambda shp: pl.BlockSpec(shp, lambda bi, ci: (0,) * len(shp))
    return pl.pallas_call(
        functools.partial(_ssd_kernel, q=q, n_pad=n_pad),
        grid=(b, s // q),
        in_specs=[
            pl.BlockSpec((None, q, D_SSM), lambda bi, ci: (bi, ci, OFF_Z // D_SSM)),
            pl.BlockSpec((None, q, CONV_DIM), lambda bi, ci: (bi, ci, OFF_XBC // CONV_DIM)),
            pl.BlockSpec((None, q, 128), lambda bi, ci: (bi, ci, OFF_DT // 128)),
            const((S_CONV, CONV_DIM)), const((1, CONV_DIM)), const((1, S_HEADS)), const((1, S_HEADS)),
            const((1, D_SSM)), const((1, D_SSM)), const((S_HEADS, D_SSM)),
            const((S_STATE, D_SSM)), const((8, CONV_DIM)),
        ],
        out_specs=[
            pl.BlockSpec((None, q, D_SSM), lambda bi, ci: (bi, ci, 0)),
            pl.BlockSpec((None, S_STATE, D_SSM), lambda bi, ci: (bi, 0, 0)),
            pl.BlockSpec((None, 8, CONV_DIM), lambda bi, ci: (bi, 0, 0)),
        ],
        out_shape=[
            jax.ShapeDtypeStruct((b, s, D_SSM), BF16),
            jax.ShapeDtypeStruct((b, S_STATE, D_SSM), F32),
            jax.ShapeDtypeStruct((b, 8, CONV_DIM), F32),
        ],
        scratch_shapes=[pltpu.VMEM((8 + q, CONV_DIM), F32), pltpu.VMEM((S_STATE, D_SSM), F32)],
        compiler_params=_cparams(("parallel", "arbitrary")),
        name="ssd",
    )(rest, rest, rest, conv_w, conv_b, dt_bias, a_log, dskip_w, norm_g, expand, state0, tail0)


def _out_kernel(x_ref, oa_ref, yg_ref, ma_ref, ms_ref, wa_ref, ws_ref, wo_ref, fg_ref, o_ref):
    y_att = jnp.dot(oa_ref[...], wa_ref[...], preferred_element_type=F32)
    y_ssm = jnp.dot(yg_ref[...], ws_ref[...], preferred_element_type=F32)
    merged = jax.nn.sigmoid(ma_ref[...]) * y_att + jax.nn.sigmoid(ms_ref[...]) * y_ssm
    hres = x_ref[...] + jnp.dot(merged.astype(BF16), wo_ref[...], preferred_element_type=F32)
    ms = jnp.mean(hres * hres, axis=-1, keepdims=True)
    o_ref[...] = hres * lax.rsqrt(ms + EPS) * fg_ref[...]


def _out(x, oa, yg, rest, w_a, w_s, w_o, final_g, tm):
    m, d = x.shape
    tm = min(tm, m)
    once = pl.Buffered(1)
    return pl.pallas_call(
        _out_kernel,
        grid=(m // tm,),
        in_specs=[
            pl.BlockSpec((tm, d), lambda i: (i, 0)),
            pl.BlockSpec((tm, V_W), lambda i: (i, 0)),
            pl.BlockSpec((tm, D_SSM), lambda i: (i, 0)),
            pl.BlockSpec((tm, d), lambda i: (i, OFF_MATT // D_MODEL)),
            pl.BlockSpec((tm, d), lambda i: (i, OFF_MSSM // D_MODEL)),
            pl.BlockSpec((V_W, d), lambda i: (0, 0), pipeline_mode=once),
            pl.BlockSpec((D_SSM, d), lambda i: (0, 0), pipeline_mode=once),
            pl.BlockSpec((d, d), lambda i: (0, 0), pipeline_mode=once),
            pl.BlockSpec((1, d), lambda i: (0, 0)),
        ],
        out_specs=pl.BlockSpec((tm, d), lambda i: (i, 0)),
        out_shape=jax.ShapeDtypeStruct((m, d), F32),
        compiler_params=_cparams(("parallel",)),
        name="out_proj",
    )(x, oa, yg, rest, rest, w_a, w_s, w_o, final_g)


def kernel(x, meta, norm_g, w_in, conv_w, conv_b, dt_bias, a_log, d_skip, ssm_norm_g,
           lam_q1, lam_k1, lam_q2, lam_k2, subln_g, w_br_attn, w_br_ssm, w_out, final_g):
    b, s, d = x.shape
    assert d == D_MODEL and norm_g.shape[0] == 1 and s % 128 == 0

    w = w_in[0]
    o_q, o_k, o_v, o_g = 0, Q_W, 2 * Q_W, 2 * Q_W + V_W
    o_z = o_g + V_W
    o_xbc = o_z + D_SSM
    o_dt = o_xbc + CONV_DIM
    o_ma = o_dt + S_HEADS
    o_ms = o_ma + D_MODEL
    w_qkv = jnp.concatenate([w[:, o_q:o_k] * (A_DK ** -0.5), w[:, o_k:o_g]], axis=1).astype(BF16)
    w_rest = jnp.concatenate([
        w[:, o_z:o_xbc], w[:, o_ma:o_ms], w[:, o_ms:o_ms + D_MODEL], w[:, o_xbc:o_dt], w[:, o_g:o_z],
        w[:, o_dt:o_ma], jnp.zeros((d, DT_W - S_HEADS), w.dtype)], axis=1).astype(BF16)
    w_kv = w_qkv[:, Q_W:]

    g_in = norm_g[0][None, :]
    xf = x.reshape(b * s, d)
    h0 = jnp.concatenate([jnp.zeros((N_PAD, d), x.dtype), meta.astype(x.dtype)], axis=0)

    qkv = _inproj(xf, g_in, w_qkv, BF16, 1024, 1024).reshape(b, s, 3 * Q_W)
    rest = _inproj(xf, g_in, w_rest, F32, 1024, 512).reshape(b, s, REST_W)
    kv0 = _inproj(h0, g_in, w_kv, BF16, PREFIX, 1024)
    rest0 = _inproj(h0, g_in, w_rest, F32, PREFIX, 512).reshape(1, PREFIX, REST_W)

    slopes = 2.0 ** (-8.0 * jnp.arange(1, A_HEADS + 1, dtype=F32) / A_HEADS)
    row = lambda v: v.astype(F32).reshape(1, -1)
    oa = _attention(slopes, row(lam_q1[0]), row(lam_k1[0]), row(lam_q2[0]), row(lam_k2[0]), row(subln_g[0]),
                    qkv, kv0, rest, 512)

    dskip_w = jnp.repeat(d_skip[0].astype(F32), S_HEADDIM)[None, :]
    expand = jnp.repeat(jnp.eye(S_HEADS, dtype=F32), S_HEADDIM, axis=1)
    ssd_args = (conv_w[0].astype(F32), row(conv_b[0]), row(dt_bias[0]), row(a_log[0]), dskip_w,
                row(ssm_norm_g[0]), expand)
    zero_state = jnp.zeros((S_STATE, D_SSM), F32)
    zero_tail = jnp.zeros((8, CONV_DIM), F32)
    _, state0, tail0 = _ssd(rest0, *ssd_args, zero_state, zero_tail, PREFIX, N_PAD)
    yg, _, _ = _ssd(rest, *ssd_args, state0[0], tail0[0], 128, 0)

    out = _out(xf, oa.reshape(b * s, V_W), yg.reshape(b * s, D_SSM), rest.reshape(b * s, REST_W),
               w_br_attn[0].astype(BF16), w_br_ssm[0].astype(BF16), w_out[0].astype(BF16),
               row(final_g), 256)
    return out.reshape(b, s, d)
```

```python
import functools
import math

import jax
import jax.numpy as jnp
from jax import lax
from jax.experimental import pallas as pl
from jax.experimental.pallas import tpu as pltpu

F32 = jnp.float32
BF16 = jnp.bfloat16

D_MODEL = 2048
N_META = 16
PREFIX = 128
N_PAD = PREFIX - N_META
A_HEADS = 8
A_DK = 64
A_DV = 128
Q_W = A_HEADS * 2 * A_DK
V_W = A_HEADS * A_DV
D_SSM = 2048
S_HEADDIM = 64
S_HEADS = D_SSM // S_HEADDIM
S_GROUPS = 4
S_STATE = 128
S_CONV = 4
GROUP_W = D_SSM // S_GROUPS
BC_W = S_GROUPS * S_STATE
CONV_DIM = D_SSM + 2 * BC_W
DT_W = 512
EPS = 1e-6
NEG = -1e30
LAM_INIT = 0.8 - 0.6 * math.exp(-0.3 * 0)

OFF_Z = 0
OFF_MATT = OFF_Z + D_SSM
OFF_MSSM = OFF_MATT + D_MODEL
OFF_XBC = OFF_MSSM + D_MODEL
OFF_G = OFF_XBC + CONV_DIM
OFF_DT = OFF_G + V_W
REST_W = OFF_DT + DT_W

VMEM_LIMIT = 56 * 1024 * 1024


def _cparams(sem):
    return pltpu.CompilerParams(dimension_semantics=sem, vmem_limit_bytes=VMEM_LIMIT)


def _inproj_kernel(x_ref, g_ref, w_ref, o_ref, u_ref):
    @pl.when(pl.program_id(1) == 0)
    def _():
        x = x_ref[...]
        ms = jnp.mean(x * x, axis=-1, keepdims=True)
        u_ref[...] = (x * lax.rsqrt(ms + EPS) * g_ref[...]).astype(BF16)

    o_ref[...] = jnp.dot(u_ref[...], w_ref[...], preferred_element_type=F32).astype(o_ref.dtype)


def _inproj(x, g, w, out_dtype, tm, tn):
    m, d = x.shape
    n = w.shape[1]
    tm = min(tm, m)
    return pl.pallas_call(
        _inproj_kernel,
        grid=(m // tm, n // tn),
        in_specs=[
            pl.BlockSpec((tm, d), lambda i, j: (i, 0)),
            pl.BlockSpec((1, d), lambda i, j: (0, 0)),
            pl.BlockSpec((d, tn), lambda i, j: (0, j)),
        ],
        out_specs=pl.BlockSpec((tm, tn), lambda i, j: (i, j)),
        out_shape=jax.ShapeDtypeStruct((m, n), out_dtype),
        scratch_shapes=[pltpu.VMEM((tm, d), BF16)],
        compiler_params=_cparams(("parallel", "arbitrary")),
        name="in_proj",
    )(x, g, w)


ONES_ROWS = 16
LOG2E = math.log2(math.e)
_NT = (((1,), (1,)), ((), ()))


def _attn_kernel(slope_ref, lq1_ref, lk1_ref, lq2_ref, lk2_ref, sg_ref,
                 q_ref, k_ref, v_ref, km_ref, vm_ref, g_ref, o_ref,
                 vt_ref, vmt_ref, bias_ref, qz_ref, sa_ref, sb_ref, m_ref, acc_ref, *, tq, tk):
    h = pl.program_id(1)
    qi = pl.program_id(2)
    slope = slope_ref[h]
    nchunk = k_ref.shape[0] // tk
    r_e = lax.broadcasted_iota(jnp.int32, (128, 128), 0)
    c_e = lax.broadcasted_iota(jnp.int32, (128, 128), 1)
    eye = (r_e == c_e).astype(BF16)
    ones = jnp.ones((ONES_ROWS, tk), BF16)

    @pl.when(qi == 0)
    def _():
        def tr(j, carry):
            off = pl.multiple_of(j * tk, tk)
            vt = lax.dot_general(eye, v_ref[pl.ds(off, tk), :], _NT, preferred_element_type=F32)
            vt_ref[j, 0:A_DV, :] = vt.astype(BF16)
            vt_ref[j, A_DV:, :] = ones
            return carry
        lax.fori_loop(0, nchunk, tr, 0)
        vmt = lax.dot_general(eye, vm_ref[...], _NT, preferred_element_type=F32)
        vmt_ref[0:A_DV, :] = vmt.astype(BF16)
        vmt_ref[A_DV:, :] = ones[:, :PREFIX]
        kk = lax.broadcasted_iota(jnp.int32, (tk, tq), 0)
        qq = lax.broadcasted_iota(jnp.int32, (tk, tq), 1)
        bias_ref[...] = -slope * (qq - kk).astype(F32)

    qt = lax.dot_general(eye, q_ref[...], _NT, preferred_element_type=F32).astype(BF16)
    drow = lax.broadcasted_iota(jnp.int32, qt.shape, 0)
    zero = jnp.zeros_like(qt)
    qz_ref[:, :tq] = jnp.where(drow < A_DK, qt, zero)
    qz_ref[:, tq:] = jnp.where(drow >= A_DK, qt, zero)

    m_ref[...] = jnp.full(m_ref.shape, NEG, F32)
    acc_ref[...] = jnp.zeros(acc_ref.shape, F32)

    def scores(k, nk):
        s = jnp.dot(k, qz_ref[...], preferred_element_type=F32)
        b = bias_ref[0:nk, :]
        return s[:, :tq] + b, s[:, tq:] + b

    def qk(kc, s_buf):
        off = pl.multiple_of(kc * tk, tk)
        s0, s1 = scores(k_ref[pl.ds(off, tk), :], tk)
        s_buf[:, :tq] = s0
        s_buf[:, tq:] = s1

    def update(s, vt, c0):
        m_old = m_ref[...]
        m_new = jnp.maximum(m_old, jnp.max(s, axis=0, keepdims=True) + c0)
        p = jnp.exp2(s - (m_new - c0)).astype(BF16)
        alpha = jnp.exp2(m_old - m_new)
        acc_ref[...] = alpha * acc_ref[...] + jnp.dot(vt, p, preferred_element_type=F32)
        m_ref[...] = m_new

    def block_offset(kc):
        return -slope * ((qi - kc) * tk).astype(F32)

    sp = jnp.concatenate(scores(km_ref[...], PREFIX), axis=1)
    krow = lax.broadcasted_iota(jnp.int32, sp.shape, 0)
    update(jnp.where(krow >= N_PAD, sp, NEG), vmt_ref[...], -slope * ((qi * tq).astype(F32) + PREFIX))

    qk(0, sa_ref)

    def pair(j, carry):
        kc = 2 * j
        qk(kc + 1, sb_ref)
        update(sa_ref[...], vt_ref[kc], block_offset(kc))
        qk(kc + 2, sa_ref)
        update(sb_ref[...], vt_ref[kc + 1], block_offset(kc + 1))
        return carry

    lax.fori_loop(0, qi // 2, pair, 0)

    def causal(s):
        kk = lax.broadcasted_iota(jnp.int32, (tk, tq), 0)
        qq = lax.broadcasted_iota(jnp.int32, (tk, tq), 1)
        keep = kk <= qq
        return jnp.concatenate([jnp.where(keep, s[:, :tq], NEG), jnp.where(keep, s[:, tq:], NEG)], axis=1)

    @pl.when(qi % 2 == 1)
    def _():
        qk(qi, sb_ref)
        update(sa_ref[...], vt_ref[qi - 1], block_offset(qi - 1))
        update(causal(sb_ref[...]), vt_ref[qi], 0.0)

    @pl.when(qi % 2 == 0)
    def _():
        update(causal(sa_ref[...]), vt_ref[qi], 0.0)

    lam =(jnp.exp(jnp.sum(lq1_ref[...] * lk1_ref[...], axis=-1, keepdims=True))
           - jnp.exp(jnp.sum(lq2_ref[...] * lk2_ref[...], axis=-1, keepdims=True)) + LAM_INIT)
    acc = acc_ref[...]
    o = acc[0:A_DV, :] / acc[A_DV:A_DV + 1, :]
    oa = (o[:, :tq] - lam * o[:, tq:]).T
    ms = jnp.mean(oa * oa, axis=-1, keepdims=True)
    oa = oa * lax.rsqrt(ms + EPS) * sg_ref[...] * (1.0 - LAM_INIT)
    g = g_ref[...]
    o_ref[...] = (oa * (g * jax.nn.sigmoid(g))).astype(o_ref.dtype)


def _attention(slopes, lq1, lk1, lq2, lk2, subln_g, qkv, kv0, rest, tq):
    b, s, _ = qkv.shape
    tq = min(tq, s)
    tk = tq
    nh = A_HEADS
    small = lambda w: pl.BlockSpec((1, w), lambda bi, hi, qi: (0, 0))
    return pl.pallas_call(
        functools.partial(_attn_kernel, tq=tq, tk=tk),
        grid=(b, nh, s // tq),
        in_specs=[
            pl.BlockSpec(memory_space=pltpu.SMEM),
            small(A_DK), small(A_DK), small(A_DK), small(A_DK), small(A_DV),
            pl.BlockSpec((None, tq, 128), lambda bi, hi, qi: (bi, qi, hi)),
            pl.BlockSpec((None, s, 128), lambda bi, hi, qi: (bi, 0, nh + hi)),
            pl.BlockSpec((None, s, 128), lambda bi, hi, qi: (bi, 0, 2 * nh + hi)),
            pl.BlockSpec((PREFIX, 128), lambda bi, hi, qi: (0, hi)),
            pl.BlockSpec((PREFIX, 128), lambda bi, hi, qi: (0, nh + hi)),
            pl.BlockSpec((None, tq, 128), lambda bi, hi, qi: (bi, qi, OFF_G // 128 + hi)),
        ],
        out_specs=pl.BlockSpec((None, tq, 128), lambda bi, hi, qi: (bi, qi, hi)),
        out_shape=jax.ShapeDtypeStruct((b, s, V_W), BF16),
        scratch_shapes=[pltpu.VMEM((s // tk, A_DV + ONES_ROWS, tk), BF16),
                        pltpu.VMEM((A_DV + ONES_ROWS, PREFIX), BF16),
                        pltpu.VMEM((tk, tq), F32),
                        pltpu.VMEM((128, 2 * tq), BF16),
                        pltpu.VMEM((tk, 2 * tq), F32),
                        pltpu.VMEM((tk, 2 * tq), F32),
                        pltpu.VMEM((1, 2 * tq), F32),
                        pltpu.VMEM((A_DV + ONES_ROWS, 2 * tq), F32)],
        compiler_params=_cparams(("parallel", "parallel", "arbitrary")),
        name="diff_attention",
    )(slopes, lq1, lk1, lq2, lk2, subln_g, qkv, qkv, qkv, kv0, kv0, rest)


def _silu(x):
    return x * jax.nn.sigmoid(x)


def _ssd_kernel(z_ref, xbc_ref, dt_ref, cw_ref, cb_ref, dtb_ref, alog_ref, dskip_ref, ng_ref,
                exp_ref, st0_ref, tail0_ref,
                y_ref, stout_ref, tailout_ref,
                hist_ref, state_ref, *, q, n_pad):
    c = pl.program_id(1)

    @pl.when(c == 0)
    def _():
        hist_ref[0:8, :] = tail0_ref[...]
        state_ref[...] = st0_ref[...]

    hist_ref[8:8 + q, :] = xbc_ref[...]
    conv = cb_ref[...]
    for j in range(S_CONV):
        conv = conv + cw_ref[j:j + 1, :] * hist_ref[8 - (S_CONV - 1) + j:8 - (S_CONV - 1) + j + q, :]
    hist_ref[0:8, :] = hist_ref[q:q + 8, :]
    xbc = _silu(conv)

    dt_raw = dt_ref[:, :S_HEADS] + dtb_ref[...]
    dt = jnp.maximum(dt_raw, 0.0) + jnp.log(1.0 + jnp.exp(-jnp.abs(dt_raw)))
    if n_pad:
        row = lax.broadcasted_iota(jnp.int32, (q, 1), 0)
        valid = row >= n_pad
        xbc = jnp.where(valid, xbc, 0.0)
        dt = jnp.where(valid, dt, 0.0)
    a = -jnp.exp(alog_ref[...])
    da = dt * a

    r_i = lax.broadcasted_iota(jnp.int32, (q, q), 0)
    c_i = lax.broadcasted_iota(jnp.int32, (q, q), 1)
    tril = c_i <= r_i
    acs = jnp.dot(tril.astype(F32), da, preferred_element_type=F32, precision=lax.Precision.HIGHEST)
    acs_t = acs.T

    wide = jnp.dot(jnp.concatenate([dt, acs], axis=0), exp_ref[...],
                   preferred_element_type=F32, precision=lax.Precision.HIGHEST)
    dt_w = wide[:q]
    acs_w = wide[q:]
    acs_last = acs_w[q - 1:q, :]

    xs = xbc[:, :D_SSM]
    bm = xbc[:, D_SSM:D_SSM + BC_W].astype(BF16)
    cm = xbc[:, D_SSM + BC_W:].astype(BF16)
    xdt = xs * dt_w
    xdt_b = xdt.astype(BF16)
    xdec_b = (xdt * jnp.exp(acs_last - acs_w)).astype(BF16)
    state_old = state_ref[...]
    state_b = state_old.astype(BF16)
    lane = lax.broadcasted_iota(jnp.int32, (q, 128), 1)

    y_groups = []
    st_parts = []
    hpg = S_HEADS // S_GROUPS
    for g in range(S_GROUPS):
        bg = bm[:, g * S_STATE:(g + 1) * S_STATE]
        cg = cm[:, g * S_STATE:(g + 1) * S_STATE]
        cbm = lax.dot_general(cg, bg, (((1,), (1,)), ((), ())), preferred_element_type=F32)
        gs = slice(g * GROUP_W, (g + 1) * GROUP_W)
        y_off = jnp.dot(cg, state_b[:, gs], preferred_element_type=F32)
        st_parts.append(lax.dot_general(bg, xdec_b[:, gs], (((0,), (0,)), ((), ())),
                                        preferred_element_type=F32))
        pairs = []
        for pair in range(hpg // 2):
            h0 = g * hpg + 2 * pair
            cols = slice(h0 * S_HEADDIM, (h0 + 2) * S_HEADDIM)
            xp = xdt_b[:, cols]
            ys = []
            for hh in (h0, h0 + 1):
                seg = acs[:, hh:hh + 1] - acs_t[hh:hh + 1, :]
                lmat = jnp.exp(jnp.where(tril, seg, -jnp.inf))
                ys.append(jnp.dot((cbm * lmat).astype(BF16), xp, preferred_element_type=F32))
            pairs.append(jnp.where(lane < S_HEADDIM, ys[0], ys[1]))
        y_groups.append(jnp.concatenate(pairs, axis=1) + y_off * jnp.exp(acs_w[:, gs]))

    y = jnp.concatenate(y_groups, axis=1)
    state_new = jnp.exp(acs_last) * state_old + jnp.concatenate(st_parts, axis=1)
    state_ref[...] = state_new

    y = y + xs * dskip_ref[...]
    z = z_ref[...]
    yg = y * _silu(z)
    outs = []
    for g in range(S_GROUPS):
        blk = yg[:, g * GROUP_W:(g + 1) * GROUP_W]
        ms = jnp.mean(blk * blk, axis=-1, keepdims=True)
        outs.append(blk * lax.rsqrt(ms + EPS))
    y_ref[...] = (jnp.concatenate(outs, axis=1) * ng_ref[...]).astype(y_ref.dtype)

    @pl.when(c == pl.num_programs(1) - 1)
    def _():
        stout_ref[...] = state_new
        tailout_ref[...] = hist_ref[0:8, :]


def _ssd(rest, conv_w, conv_b, dt_bias, a_log, dskip_w, norm_g, expand, state0, tail0, q, n_pad):
    b, s, _ = rest.shape
    q = min(q, s)
    const = lambda shp: pl.BlockSpec(shp, lambda bi, ci: (0,) * len(shp))
    return pl.pallas_call(
        functools.partial(_ssd_kernel, q=q, n_pad=n_pad),
        grid=(b, s // q),
        in_specs=[
            pl.BlockSpec((None, q, D_SSM), lambda bi, ci: (bi, ci, OFF_Z // D_SSM)),
            pl.BlockSpec((None, q, CONV_DIM), lambda bi, ci: (bi, ci, OFF_XBC // CONV_DIM)),
            pl.BlockSpec((None, q, 128), lambda bi, ci: (bi, ci, OFF_DT // 128)),
            const((S_CONV, CONV_DIM)), const((1, CONV_DIM)), const((1, S_HEADS)), const((1, S_HEADS)),
            const((1, D_SSM)), const((1, D_SSM)), const((S_HEADS, D_SSM)),
            const((S_STATE, D_SSM)), const((8, CONV_DIM)),
        ],
        out_specs=[
            pl.BlockSpec((None, q, D_SSM), lambda bi, ci: (bi, ci, 0)),
            pl.BlockSpec((None, S_STATE, D_SSM), lambda bi, ci: (bi, 0, 0)),
            pl.BlockSpec((None, 8, CONV_DIM), lambda bi, ci: (bi, 0, 0)),
        ],
        out_shape=[
            jax.ShapeDtypeStruct((b, s, D_SSM), BF16),
            jax.ShapeDtypeStruct((b, S_STATE, D_SSM), F32),
            jax.ShapeDtypeStruct((b, 8, CONV_DIM), F32),
        ],
        scratch_shapes=[pltpu.VMEM((8 + q, CONV_DIM), F32), pltpu.VMEM((S_STATE, D_SSM), F32)],
        compiler_params=_cparams(("parallel", "arbitrary")),
        name="ssd",
    )(rest, rest, rest, conv_w, conv_b, dt_bias, a_log, dskip_w, norm_g, expand, state0, tail0)


def _out_kernel(x_ref, oa_ref, yg_ref, ma_ref, ms_ref, wa_ref, ws_ref, wo_ref, fg_ref, o_ref):
    y_att = jnp.dot(oa_ref[...], wa_ref[...], preferred_element_type=F32)
    y_ssm = jnp.dot(yg_ref[...], ws_ref[...], preferred_element_type=F32)
    merged = jax.nn.sigmoid(ma_ref[...]) * y_att + jax.nn.sigmoid(ms_ref[...]) * y_ssm
    hres = x_ref[...] + jnp.dot(merged.astype(BF16), wo_ref[...], preferred_element_type=F32)
    ms = jnp.mean(hres * hres, axis=-1, keepdims=True)
    o_ref[...] = hres * lax.rsqrt(ms + EPS) * fg_ref[...]


def _out(x, oa, yg, rest, w_a, w_s, w_o, final_g, tm):
    m, d = x.shape
    tm = min(tm, m)
    once = pl.Buffered(1)
    return pl.pallas_call(
        _out_kernel,
        grid=(m // tm,),
        in_specs=[
            pl.BlockSpec((tm, d), lambda i: (i, 0)),
            pl.BlockSpec((tm, V_W), lambda i: (i, 0)),
            pl.BlockSpec((tm, D_SSM), lambda i: (i, 0)),
            pl.BlockSpec((tm, d), lambda i: (i, OFF_MATT // D_MODEL)),
            pl.BlockSpec((tm, d), lambda i: (i, OFF_MSSM // D_MODEL)),
            pl.BlockSpec((V_W, d), lambda i: (0, 0), pipeline_mode=once),
            pl.BlockSpec((D_SSM, d), lambda i: (0, 0), pipeline_mode=once),
            pl.BlockSpec((d, d), lambda i: (0, 0), pipeline_mode=once),
            pl.BlockSpec((1, d), lambda i: (0, 0)),
        ],
        out_specs=pl.BlockSpec((tm, d), lambda i: (i, 0)),
        out_shape=jax.ShapeDtypeStruct((m, d), F32),
        compiler_params=_cparams(("parallel",)),
        name="out_proj",
    )(x, oa, yg, rest, rest, w_a, w_s, w_o, final_g)


def kernel(x, meta, norm_g, w_in, conv_w, conv_b, dt_bias, a_log, d_skip, ssm_norm_g,
           lam_q1, lam_k1, lam_q2, lam_k2, subln_g, w_br_attn, w_br_ssm, w_out, final_g):
    b, s, d = x.shape
    assert d == D_MODEL and norm_g.shape[0] == 1 and s % 128 == 0

    w = w_in[0]
    o_q, o_k, o_v, o_g = 0, Q_W, 2 * Q_W, 2 * Q_W + V_W
    o_z = o_g + V_W
    o_xbc = o_z + D_SSM
    o_dt = o_xbc + CONV_DIM
    o_ma = o_dt + S_HEADS
    o_ms = o_ma + D_MODEL
    w_qkv = jnp.concatenate([w[:, o_q:o_k] * (A_DK ** -0.5 * LOG2E), w[:, o_k:o_g]], axis=1).astype(BF16)
    w_rest = jnp.concatenate([
        w[:, o_z:o_xbc], w[:, o_ma:o_ms], w[:, o_ms:o_ms + D_MODEL], w[:, o_xbc:o_dt], w[:, o_g:o_z],
        w[:, o_dt:o_ma], jnp.zeros((d, DT_W - S_HEADS), w.dtype)], axis=1).astype(BF16)
    w_kv = w_qkv[:, Q_W:]

    g_in = norm_g[0][None, :]
    xf = x.reshape(b * s, d)
    h0 = jnp.concatenate([jnp.zeros((N_PAD, d), x.dtype), meta.astype(x.dtype)], axis=0)

    qkv = _inproj(xf, g_in, w_qkv, BF16, 1024, 1536).reshape(b, s, 3 * Q_W)
    rest = _inproj(xf, g_in, w_rest, F32, 1024, 1536).reshape(b, s, REST_W)
    kv0 = _inproj(h0, g_in, w_kv, BF16, PREFIX, 1024)
    rest0 = _inproj(h0, g_in, w_rest, F32, PREFIX, 1536).reshape(1, PREFIX, REST_W)

    slopes = LOG2E * 2.0 ** (-8.0 * jnp.arange(1, A_HEADS + 1, dtype=F32) / A_HEADS)
    row = lambda v: v.astype(F32).reshape(1, -1)
    oa = _attention(slopes, row(lam_q1[0]), row(lam_k1[0]), row(lam_q2[0]), row(lam_k2[0]), row(subln_g[0]),
                    qkv, kv0, rest, 512)

    dskip_w = jnp.repeat(d_skip[0].astype(F32), S_HEADDIM)[None, :]
    expand = jnp.repeat(jnp.eye(S_HEADS, dtype=F32), S_HEADDIM, axis=1)
    ssd_args = (conv_w[0].astype(F32), row(conv_b[0]), row(dt_bias[0]), row(a_log[0]), dskip_w,
                row(ssm_norm_g[0]), expand)
    zero_state = jnp.zeros((S_STATE, D_SSM), F32)
    zero_tail = jnp.zeros((8, CONV_DIM), F32)
    _, state0, tail0 = _ssd(rest0, *ssd_args, zero_state, zero_tail, PREFIX, N_PAD)
    yg, _, _ = _ssd(rest, *ssd_args, state0[0], tail0[0], 128, 0)

    out = _out(xf, oa.reshape(b * s, V_W), yg.reshape(b * s, D_SSM), rest.reshape(b * s, REST_W),
               w_br_attn[0].astype(BF16), w_br_ssm[0].astype(BF16), w_out[0].astype(BF16),
               row(final_g), 256)
    return out.reshape(b, s, d)
```

```python
import functools
import math

import jax
import jax.numpy as jnp
import numpy as np
from jax import lax
from jax.experimental import pallas as pl
from jax.experimental.pallas import tpu as pltpu

F32 = jnp.float32
BF16 = jnp.bfloat16

D_MODEL = 2048
N_META = 16
PREFIX = 128
N_PAD = PREFIX - N_META
A_HEADS = 8
A_DK = 64
A_DV = 128
Q_W = A_HEADS * 2 * A_DK
V_W = A_HEADS * A_DV
D_SSM = 2048
S_HEADDIM = 64
S_HEADS = D_SSM // S_HEADDIM
S_GROUPS = 4
S_STATE = 128
S_CONV = 4
GROUP_W = D_SSM // S_GROUPS
BC_W = S_GROUPS * S_STATE
CONV_DIM = D_SSM + 2 * BC_W
DT_W = 512
EPS = 1e-6
NEG = -1e30
LAM_INIT = 0.8 - 0.6 * math.exp(-0.3 * 0)

OFF_Z = 0
OFF_MATT = OFF_Z + D_SSM
OFF_MSSM = OFF_MATT + D_MODEL
OFF_XBC = OFF_MSSM + D_MODEL
OFF_G = OFF_XBC + CONV_DIM
OFF_DT = OFF_G + V_W
REST_W = OFF_DT + DT_W

VMEM_LIMIT = 56 * 1024 * 1024


def _cparams(sem):
    return pltpu.CompilerParams(dimension_semantics=sem, vmem_limit_bytes=VMEM_LIMIT)


def _inproj_kernel(x_ref, g_ref, w_ref, o_ref, u_ref):
    @pl.when(pl.program_id(1) == 0)
    def _():
        x = x_ref[...]
        ms = jnp.mean(x * x, axis=-1, keepdims=True)
        u_ref[...] = (x * lax.rsqrt(ms + EPS) * g_ref[...]).astype(BF16)

    o_ref[...] = jnp.dot(u_ref[...], w_ref[...], preferred_element_type=F32).astype(o_ref.dtype)


def _inproj(x, g, w, out_dtype, tm, tn):
    m, d = x.shape
    n = w.shape[1]
    tm = min(tm, m)
    return pl.pallas_call(
        _inproj_kernel,
        grid=(m // tm, n // tn),
        in_specs=[
            pl.BlockSpec((tm, d), lambda i, j: (i, 0)),
            pl.BlockSpec((1, d), lambda i, j: (0, 0)),
            pl.BlockSpec((d, tn), lambda i, j: (0, j)),
        ],
        out_specs=pl.BlockSpec((tm, tn), lambda i, j: (i, j)),
        out_shape=jax.ShapeDtypeStruct((m, n), out_dtype),
        scratch_shapes=[pltpu.VMEM((tm, d), BF16)],
        compiler_params=_cparams(("parallel", "arbitrary")),
        name="in_proj",
    )(x, g, w)


ONES_ROWS = 16
LOG2E = math.log2(math.e)
_NT = (((1,), (1,)), ((), ()))


def _attn_kernel(slope_ref, lq1_ref, lk1_ref, lq2_ref, lk2_ref, sg_ref,
                 q_ref, k_ref, v_ref, km_ref, vm_ref, g_ref, o_ref,
                 vt_ref, vmt_ref, kpos_ref, qz_ref, sa_ref, sb_ref, cmax_a, cmax_b, m_ref, acc_ref,
                 *, tq, tk):
    h = pl.program_id(1)
    qi = pl.program_id(2)
    slope = slope_ref[h, 0]
    nchunk = k_ref.shape[0] // tk
    r_e = lax.broadcasted_iota(jnp.int32, (128, 128), 0)
    c_e = lax.broadcasted_iota(jnp.int32, (128, 128), 1)
    eye = (r_e == c_e).astype(BF16)

    @pl.when(qi == 0)
    def _():
        ones = jnp.ones((ONES_ROWS, tk), BF16)

        def tr(j, carry):
            off = pl.multiple_of(j * tk, tk)
            vt = lax.dot_general(eye, v_ref[pl.ds(off, tk), :], _NT, preferred_element_type=F32)
            vt_ref[j, 0:A_DV, :] = vt.astype(BF16)
            vt_ref[j, A_DV:, :] = ones
            return carry
        lax.fori_loop(0, nchunk, tr, 0)
        vmt = lax.dot_general(eye, vm_ref[...], _NT, preferred_element_type=F32)
        vmt_ref[0:A_DV, :] = vmt.astype(BF16)
        vmt_ref[A_DV:, :] = ones[:, :PREFIX]

        row = lax.broadcasted_iota(jnp.int32, (tk, 128), 0)
        lane = lax.broadcasted_iota(jnp.int32, (tk, 128), 1)
        lo = row % 256
        kpos = jnp.where(lane < 3, lo, jnp.where(lane < 6, row - lo, 0))
        kpos_ref[...] = kpos.astype(F32).astype(BF16)
        arow = lax.broadcasted_iota(jnp.int32, (128, 2 * tq), 0)
        piece = jnp.where(arow % 3 == 0, slope_ref[h, 1], jnp.where(arow % 3 == 1, slope_ref[h, 2], slope_ref[h, 3]))
        qz_ref[128:256, :] = jnp.where(arow < 6, piece, 0.0).astype(BF16)

    qt = lax.dot_general(eye, q_ref[...], _NT, preferred_element_type=F32).astype(BF16)
    drow = lax.broadcasted_iota(jnp.int32, qt.shape, 0)
    zero = jnp.zeros_like(qt)
    qz_ref[0:128, :tq] = jnp.where(drow < A_DK, qt, zero)
    qz_ref[0:128, tq:] = jnp.where(drow >= A_DK, qt, zero)

    m_ref[...] = jnp.full(m_ref.shape, NEG, F32)
    acc_ref[...] = jnp.zeros(acc_ref.shape, F32)

    def scores(k, nk):
        lhs = jnp.concatenate([k, kpos_ref[0:nk, :]], axis=1)
        return jnp.dot(lhs, qz_ref[...], preferred_element_type=F32)

    def qk(kc, s_buf, cmax_ref):
        off = pl.multiple_of(kc * tk, tk)
        s = scores(k_ref[pl.ds(off, tk), :], tk)
        s_buf[...] = s
        cmax_ref[...] = jnp.max(s, axis=0, keepdims=True)

    def update(s, cmax, vt, c0):
        m_old = m_ref[...]
        m_new = jnp.maximum(m_old, cmax + c0)
        p = jnp.exp2(s - (m_new - c0)).astype(BF16)
        alpha = jnp.exp2(m_old - m_new)
        acc_ref[...] = alpha * acc_ref[...] + jnp.dot(vt, p, preferred_element_type=F32)
        m_ref[...] = m_new

    def masked_update(s, keep, vt, c0):
        s = jnp.where(jnp.concatenate([keep, keep], axis=1), s, NEG)
        update(s, jnp.max(s, axis=0, keepdims=True), vt, c0)

    def chunk_offset(kc):
        return -slope * ((qi - kc) * tk).astype(F32)

    s_prefix = scores(km_ref[...], PREFIX)

    qk(0, sa_ref, cmax_a)

    krow = lax.broadcasted_iota(jnp.int32, (PREFIX, tq), 0)
    masked_update(s_prefix, krow >= N_PAD, vmt_ref[...], -slope * ((qi * tq).astype(F32) + PREFIX))

    def pair(j, carry):
        kc = 2 * j
        qk(kc + 1, sb_ref, cmax_b)
        update(sa_ref[...], cmax_a[...], vt_ref[kc], chunk_offset(kc))
        qk(kc + 2, sa_ref, cmax_a)
        update(sb_ref[...], cmax_b[...], vt_ref[kc + 1], chunk_offset(kc + 1))
        return carry

    lax.fori_loop(0, qi // 2, pair, 0)

    kk = lax.broadcasted_iota(jnp.int32, (tk, tq), 0)
    qq = lax.broadcasted_iota(jnp.int32, (tk, tq), 1)
    causal = kk <= qq

    @pl.when(qi % 2 == 1)
    def _():
        qk(qi, sb_ref, cmax_b)
        update(sa_ref[...], cmax_a[...], vt_ref[qi - 1], chunk_offset(qi - 1))
        masked_update(sb_ref[...], causal, vt_ref[qi], 0.0)

    @pl.when(qi % 2 == 0)
    def _():
        masked_update(sa_ref[...], causal, vt_ref[qi], 0.0)

    lam =(jnp.exp(jnp.sum(lq1_ref[...] * lk1_ref[...], axis=-1, keepdims=True))
           - jnp.exp(jnp.sum(lq2_ref[...] * lk2_ref[...], axis=-1, keepdims=True)) + LAM_INIT)
    acc = acc_ref[...]
    o = acc[0:A_DV, :] / acc[A_DV:A_DV + 1, :]
    oa = (o[:, :tq] - lam * o[:, tq:]).T
    ms = jnp.mean(oa * oa, axis=-1, keepdims=True)
    oa = oa * lax.rsqrt(ms + EPS) * sg_ref[...] * (1.0 - LAM_INIT)
    g = g_ref[...]
    o_ref[...] = (oa * (g * jax.nn.sigmoid(g))).astype(o_ref.dtype)


def _slope_table():
    slope2 = (LOG2E * 2.0 ** (-8.0 * np.arange(1, A_HEADS + 1) / A_HEADS)).astype(np.float32)

    def top16(v):
        return (v.astype(np.float32).view(np.uint32) & np.uint32(0xFFFF0000)).view(np.float32)

    b1 = top16(slope2)
    b2 = top16(slope2 - b1)
    b3 = top16(slope2 - b1 - b2)
    return np.stack([slope2, b1, b2, b3], axis=1)


def _attention(slopes, lq1, lk1, lq2, lk2, subln_g, qkv, kv0, rest, tq):
    b, s, _ = qkv.shape
    tq = min(tq, s)
    tk = tq
    nh = A_HEADS
    small = lambda w: pl.BlockSpec((1, w), lambda bi, hi, qi: (0, 0))
    return pl.pallas_call(
        functools.partial(_attn_kernel, tq=tq, tk=tk),
        grid=(b, nh, s // tq),
        in_specs=[
            pl.BlockSpec(memory_space=pltpu.SMEM),
            small(A_DK), small(A_DK), small(A_DK), small(A_DK), small(A_DV),
            pl.BlockSpec((None, tq, 128), lambda bi, hi, qi: (bi, qi, hi)),
            pl.BlockSpec((None, s, 128), lambda bi, hi, qi: (bi, 0, nh + hi)),
            pl.BlockSpec((None, s, 128), lambda bi, hi, qi: (bi, 0, 2 * nh + hi)),
            pl.BlockSpec((PREFIX, 128), lambda bi, hi, qi: (0, hi)),
            pl.BlockSpec((PREFIX, 128), lambda bi, hi, qi: (0, nh + hi)),
            pl.BlockSpec((None, tq, 128), lambda bi, hi, qi: (bi, qi, OFF_G // 128 + hi)),
        ],
        out_specs=pl.BlockSpec((None, tq, 128), lambda bi, hi, qi: (bi, qi, hi)),
        out_shape=jax.ShapeDtypeStruct((b, s, V_W), BF16),
        scratch_shapes=[pltpu.VMEM((s // tk, A_DV + ONES_ROWS, tk), BF16),
                        pltpu.VMEM((A_DV + ONES_ROWS, PREFIX), BF16),
                        pltpu.VMEM((tk, 128), BF16),
                        pltpu.VMEM((256, 2 * tq), BF16),
                        pltpu.VMEM((tk, 2 * tq), F32),
                        pltpu.VMEM((tk, 2 * tq), F32),
                        pltpu.VMEM((1, 2 * tq), F32),
                        pltpu.VMEM((1, 2 * tq), F32),
                        pltpu.VMEM((1, 2 * tq), F32),
                        pltpu.VMEM((A_DV + ONES_ROWS, 2 * tq), F32)],
        compiler_params=_cparams(("parallel", "parallel", "arbitrary")),
        name="diff_attention",
    )(slopes, lq1, lk1, lq2, lk2, subln_g, qkv, qkv, qkv, kv0, kv0, rest)


def _silu(x):
    return x * jax.nn.sigmoid(x)


def _split3(x):
    def top16(v):
        bits = lax.bitcast_convert_type(v, jnp.uint32) & jnp.uint32(0xFFFF0000)
        return lax.bitcast_convert_type(bits, F32)
    p1 = top16(x)
    r1 = x - p1
    p2 = top16(r1)
    return p1.astype(BF16), p2.astype(BF16), (r1 - p2).astype(BF16)


def _ssd_kernel(z_ref, xbc_ref, dt_ref, cw_ref, cb_ref, dtb_ref, alog_ref, dskip_ref, ng_ref,
                exp_ref, st0_ref, tail0_ref,
                y_ref, stout_ref, tailout_ref,
                hist_ref, state_ref, *, q, n_pad):
    c = pl.program_id(1)

    @pl.when(c == 0)
    def _():
        hist_ref[0:8, :] = tail0_ref[...]
        state_ref[...] = st0_ref[...]

    hist_ref[8:8 + q, :] = xbc_ref[...]
    conv = cb_ref[...]
    for j in range(S_CONV):
        conv = conv + cw_ref[j:j + 1, :] * hist_ref[8 - (S_CONV - 1) + j:8 - (S_CONV - 1) + j + q, :]
    hist_ref[0:8, :] = hist_ref[q:q + 8, :]
    xbc = _silu(conv)

    dt_raw = dt_ref[:, :S_HEADS] + dtb_ref[...]
    dt = jnp.maximum(dt_raw, 0.0) + jnp.log(1.0 + jnp.exp(-jnp.abs(dt_raw)))
    if n_pad:
        row = lax.broadcasted_iota(jnp.int32, (q, 1), 0)
        valid = row >= n_pad
        xbc = jnp.where(valid, xbc, 0.0)
        dt = jnp.where(valid, dt, 0.0)
    a = -jnp.exp(alog_ref[...])
    da = dt * a

    r_i = lax.broadcasted_iota(jnp.int32, (q, q), 0)
    c_i = lax.broadcasted_iota(jnp.int32, (q, q), 1)
    tril = c_i <= r_i
    acs = jnp.dot(tril.astype(F32), da, preferred_element_type=F32, precision=lax.Precision.HIGHEST)
    acs_t = acs.T

    pieces = _split3(jnp.concatenate([dt, acs], axis=0))
    wide = jnp.dot(jnp.concatenate(pieces, axis=1), exp_ref[...], preferred_element_type=F32)
    dt_w = wide[:q]
    acs_w = wide[q:]
    acs_last = acs_w[q - 1:q, :]

    xs = xbc[:, :D_SSM]
    bm = xbc[:, D_SSM:D_SSM + BC_W].astype(BF16)
    cm = xbc[:, D_SSM + BC_W:].astype(BF16)
    xdt = xs * dt_w
    xdt_b = xdt.astype(BF16)
    xdec_b = (xdt * jnp.exp(acs_last - acs_w)).astype(BF16)
    state_old = state_ref[...]
    state_b = state_old.astype(BF16)
    lane = lax.broadcasted_iota(jnp.int32, (q, 128), 1)

    y_groups = []
    st_parts = []
    hpg = S_HEADS // S_GROUPS
    for g in range(S_GROUPS):
        bg = bm[:, g * S_STATE:(g + 1) * S_STATE]
        cg = cm[:, g * S_STATE:(g + 1) * S_STATE]
        cbm = lax.dot_general(cg, bg, (((1,), (1,)), ((), ())), preferred_element_type=F32)
        gs = slice(g * GROUP_W, (g + 1) * GROUP_W)
        y_off = jnp.dot(cg, state_b[:, gs], preferred_element_type=F32)
        st_parts.append(lax.dot_general(bg, xdec_b[:, gs], (((0,), (0,)), ((), ())),
                                        preferred_element_type=F32))
        pairs = []
        for pair in range(hpg // 2):
            h0 = g * hpg + 2 * pair
            cols = slice(h0 * S_HEADDIM, (h0 + 2) * S_HEADDIM)
            xp = xdt_b[:, cols]
            ys = []
            for hh in (h0, h0 + 1):
                seg = acs[:, hh:hh + 1] - acs_t[hh:hh + 1, :]
                lmat = jnp.exp(jnp.where(tril, seg, -jnp.inf))
                ys.append(jnp.dot((cbm * lmat).astype(BF16), xp, preferred_element_type=F32))
            pairs.append(jnp.where(lane < S_HEADDIM, ys[0], ys[1]))
        y_groups.append(jnp.concatenate(pairs, axis=1) + y_off * jnp.exp(acs_w[:, gs]))

    y = jnp.concatenate(y_groups, axis=1)
    state_new = jnp.exp(acs_last) * state_old + jnp.concatenate(st_parts, axis=1)
    state_ref[...] = state_new

    y = y + xs * dskip_ref[...]
    z = z_ref[...]
    yg = y * _silu(z)
    outs = []
    for g in range(S_GROUPS):
        blk = yg[:, g * GROUP_W:(g + 1) * GROUP_W]
        ms = jnp.mean(blk * blk, axis=-1, keepdims=True)
        outs.append(blk * lax.rsqrt(ms + EPS))
    y_ref[...] = (jnp.concatenate(outs, axis=1) * ng_ref[...]).astype(y_ref.dtype)

    @pl.when(c == pl.num_programs(1) - 1)
    def _():
        stout_ref[...] = state_new
        tailout_ref[...] = hist_ref[0:8, :]


def _ssd(rest, conv_w, conv_b, dt_bias, a_log, dskip_w, norm_g, expand, state0, tail0, q, n_pad):
    b, s, _ = rest.shape
    q = min(q, s)
    const = lambda shp: pl.BlockSpec(shp, lambda bi, ci: (0,) * len(shp))
    return pl.pallas_call(
        functools.partial(_ssd_kernel, q=q, n_pad=n_pad),
        grid=(b, s // q),
        in_specs=[
            pl.BlockSpec((None, q, D_SSM), lambda bi, ci: (bi, ci, OFF_Z // D_SSM)),
            pl.BlockSpec((None, q, CONV_DIM), lambda bi, ci: (bi, ci, OFF_XBC // CONV_DIM)),
            pl.BlockSpec((None, q, 128), lambda bi, ci: (bi, ci, OFF_DT // 128)),
            const((S_CONV, CONV_DIM)), const((1, CONV_DIM)), const((1, S_HEADS)), const((1, S_HEADS)),
            const((1, D_SSM)), const((1, D_SSM)), const((3 * S_HEADS, D_SSM)),
            const((S_STATE, D_SSM)), const((8, CONV_DIM)),
        ],
        out_specs=[
            pl.BlockSpec((None, q, D_SSM), lambda bi, ci: (bi, ci, 0)),
            pl.BlockSpec((None, S_STATE, D_SSM), lambda bi, ci: (bi, 0, 0)),
            pl.BlockSpec((None, 8, CONV_DIM), lambda bi, ci: (bi, 0, 0)),
        ],
        out_shape=[
            jax.ShapeDtypeStruct((b, s, D_SSM), BF16),
            jax.ShapeDtypeStruct((b, S_STATE, D_SSM), F32),
            jax.ShapeDtypeStruct((b, 8, CONV_DIM), F32),
        ],
        scratch_shapes=[pltpu.VMEM((8 + q, CONV_DIM), F32), pltpu.VMEM((S_STATE, D_SSM), F32)],
        compiler_params=_cparams(("parallel", "arbitrary")),
        name="ssd",
    )(rest, rest, rest, conv_w, conv_b, dt_bias, a_log, dskip_w, norm_g, expand, state0, tail0)


def _out_kernel(x_ref, oa_ref, yg_ref, ma_ref, ms_ref, wa_ref, ws_ref, wo_ref, fg_ref, o_ref):
    y_att = jnp.dot(oa_ref[...], wa_ref[...], preferred_element_type=F32)
    y_ssm = jnp.dot(yg_ref[...], ws_ref[...], preferred_element_type=F32)
    merged = jax.nn.sigmoid(ma_ref[...]) * y_att + jax.nn.sigmoid(ms_ref[...]) * y_ssm
    hres = x_ref[...] + jnp.dot(merged.astype(BF16), wo_ref[...], preferred_element_type=F32)
    ms = jnp.mean(hres * hres, axis=-1, keepdims=True)
    o_ref[...] = hres * lax.rsqrt(ms + EPS) * fg_ref[...]


def _out(x, oa, yg, rest, w_a, w_s, w_o, final_g, tm):
    m, d = x.shape
    tm = min(tm, m)
    once = pl.Buffered(1)
    return pl.pallas_call(
        _out_kernel,
        grid=(m // tm,),
        in_specs=[
            pl.BlockSpec((tm, d), lambda i: (i, 0)),
            pl.BlockSpec((tm, V_W), lambda i: (i, 0)),
            pl.BlockSpec((tm, D_SSM), lambda i: (i, 0)),
            pl.BlockSpec((tm, d), lambda i: (i, OFF_MATT // D_MODEL)),
            pl.BlockSpec((tm, d), lambda i: (i, OFF_MSSM // D_MODEL)),
            pl.BlockSpec((V_W, d), lambda i: (0, 0), pipeline_mode=once),
            pl.BlockSpec((D_SSM, d), lambda i: (0, 0), pipeline_mode=once),
            pl.BlockSpec((d, d), lambda i: (0, 0), pipeline_mode=once),
            pl.BlockSpec((1, d), lambda i: (0, 0)),
        ],
        out_specs=pl.BlockSpec((tm, d), lambda i: (i, 0)),
        out_shape=jax.ShapeDtypeStruct((m, d), F32),
        compiler_params=_cparams(("parallel",)),
        name="out_proj",
    )(x, oa, yg, rest, rest, w_a, w_s, w_o, final_g)


def kernel(x, meta, norm_g, w_in, conv_w, conv_b, dt_bias, a_log, d_skip, ssm_norm_g,
           lam_q1, lam_k1, lam_q2, lam_k2, subln_g, w_br_attn, w_br_ssm, w_out, final_g):
    b, s, d = x.shape
    assert d == D_MODEL and norm_g.shape[0] == 1 and s % 128 == 0

    w = w_in[0]
    o_q, o_k, o_v, o_g = 0, Q_W, 2 * Q_W, 2 * Q_W + V_W
    o_z = o_g + V_W
    o_xbc = o_z + D_SSM
    o_dt = o_xbc + CONV_DIM
    o_ma = o_dt + S_HEADS
    o_ms = o_ma + D_MODEL
    w_qkv = jnp.concatenate([w[:, o_q:o_k] * (A_DK ** -0.5 * LOG2E), w[:, o_k:o_g]], axis=1).astype(BF16)
    w_rest = jnp.concatenate([
        w[:, o_z:o_xbc], w[:, o_ma:o_ms], w[:, o_ms:o_ms + D_MODEL], w[:, o_xbc:o_dt], w[:, o_g:o_z],
        w[:, o_dt:o_ma], jnp.zeros((d, DT_W - S_HEADS), w.dtype)], axis=1).astype(BF16)
    w_kv = w_qkv[:, Q_W:]

    g_in = norm_g[0][None, :]
    xf = x.reshape(b * s, d)
    h0 = jnp.concatenate([jnp.zeros((N_PAD, d), x.dtype), meta.astype(x.dtype)], axis=0)

    qkv = _inproj(xf, g_in, w_qkv, BF16, 1024, 1536).reshape(b, s, 3 * Q_W)
    rest = _inproj(xf, g_in, w_rest, F32, 1024, 1536).reshape(b, s, REST_W)
    kv0 = _inproj(h0, g_in, w_kv, BF16, PREFIX, 1024)
    rest0 = _inproj(h0, g_in, w_rest, F32, PREFIX, 1536).reshape(1, PREFIX, REST_W)

    slopes = jnp.asarray(_slope_table())
    row = lambda v: v.astype(F32).reshape(1, -1)
    oa = _attention(slopes, row(lam_q1[0]), row(lam_k1[0]), row(lam_q2[0]), row(lam_k2[0]), row(subln_g[0]),
                    qkv, kv0, rest, 512)

    dskip_w = jnp.repeat(d_skip[0].astype(F32), S_HEADDIM)[None, :]
    expand = jnp.tile(jnp.repeat(jnp.eye(S_HEADS, dtype=BF16), S_HEADDIM, axis=1), (3, 1))
    ssd_args = (conv_w[0].astype(F32), row(conv_b[0]), row(dt_bias[0]), row(a_log[0]), dskip_w,
                row(ssm_norm_g[0]), expand)
    zero_state = jnp.zeros((S_STATE, D_SSM), F32)
    zero_tail = jnp.zeros((8, CONV_DIM), F32)
    _, state0, tail0 = _ssd(rest0, *ssd_args, zero_state, zero_tail, PREFIX, N_PAD)
    yg, _, _ = _ssd(rest, *ssd_args, state0[0], tail0[0], 128, 0)

    out = _out(xf, oa.reshape(b * s, V_W), yg.reshape(b * s, D_SSM), rest.reshape(b * s, REST_W),
               w_br_attn[0].astype(BF16), w_br_ssm[0].astype(BF16), w_out[0].astype(BF16),
               row(final_g), 256)
    return out.reshape(b, s, d)
```

```python
import functools
import math

import jax
import jax.numpy as jnp
import numpy as np
from jax import lax
from jax.experimental import pallas as pl
from jax.experimental.pallas import tpu as pltpu

F32 = jnp.float32
BF16 = jnp.bfloat16

D_MODEL = 2048
N_META = 16
PREFIX = 128
N_PAD = PREFIX - N_META
A_HEADS = 8
A_DK = 64
A_DV = 128
Q_W = A_HEADS * 2 * A_DK
V_W = A_HEADS * A_DV
D_SSM = 2048
S_HEADDIM = 64
S_HEADS = D_SSM // S_HEADDIM
S_GROUPS = 4
S_STATE = 128
S_CONV = 4
GROUP_W = D_SSM // S_GROUPS
BC_W = S_GROUPS * S_STATE
CONV_DIM = D_SSM + 2 * BC_W
DT_W = 512
EPS = 1e-6
NEG = -1e30
LAM_INIT = 0.8 - 0.6 * math.exp(-0.3 * 0)

OFF_Z = 0
OFF_MATT = OFF_Z + D_SSM
OFF_MSSM = OFF_MATT + D_MODEL
OFF_XBC = OFF_MSSM + D_MODEL
OFF_G = OFF_XBC + CONV_DIM
OFF_DT = OFF_G + V_W
REST_W = OFF_DT + DT_W

VMEM_LIMIT = 56 * 1024 * 1024


def _cparams(sem):
    return pltpu.CompilerParams(dimension_semantics=sem, vmem_limit_bytes=VMEM_LIMIT)


def _inproj_kernel(x_ref, g_ref, w_ref, o_ref, u_ref):
    @pl.when(pl.program_id(1) == 0)
    def _():
        x = x_ref[...]
        ms = jnp.mean(x * x, axis=-1, keepdims=True)
        u_ref[...] = (x * lax.rsqrt(ms + EPS) * g_ref[...]).astype(BF16)

    o_ref[...] = lax.dot_general(u_ref[...], w_ref[...], _NT, preferred_element_type=F32).astype(o_ref.dtype)


def _inproj(x, g, w, out_dtype, tm, tn):
    m, d = x.shape
    n = w.shape[0]
    tm = min(tm, m)
    return pl.pallas_call(
        _inproj_kernel,
        grid=(m // tm, n // tn),
        in_specs=[
            pl.BlockSpec((tm, d), lambda i, j: (i, 0)),
            pl.BlockSpec((1, d), lambda i, j: (0, 0)),
            pl.BlockSpec((tn, d), lambda i, j: (j, 0)),
        ],
        out_specs=pl.BlockSpec((tm, tn), lambda i, j: (i, j)),
        out_shape=jax.ShapeDtypeStruct((m, n), out_dtype),
        scratch_shapes=[pltpu.VMEM((tm, d), BF16)],
        compiler_params=_cparams(("parallel", "arbitrary")),
        name="in_proj",
    )(x, g, w)


ONES_ROWS = 16
LOG2E = math.log2(math.e)
_NT = (((1,), (1,)), ((), ()))


def _attn_kernel(slope_ref, lq1_ref, lk1_ref, lq2_ref, lk2_ref, sg_ref,
                 q_ref, k_ref, v_ref, km_ref, vm_ref, g_ref, o_ref,
                 vt_ref, vmt_ref, kpos_ref, qz_ref, sa_ref, sb_ref, cmax_a, cmax_b, m_ref, acc_ref,
                 *, tq, tk):
    h = pl.program_id(1)
    qi = pl.program_id(2)
    slope = slope_ref[h, 0]
    nchunk = k_ref.shape[0] // tk
    r_e = lax.broadcasted_iota(jnp.int32, (128, 128), 0)
    c_e = lax.broadcasted_iota(jnp.int32, (128, 128), 1)
    eye = (r_e == c_e).astype(BF16)

    @pl.when(qi == 0)
    def _():
        ones = jnp.ones((ONES_ROWS, tk), BF16)

        def tr(j, carry):
            off = pl.multiple_of(j * tk, tk)
            vt = lax.dot_general(eye, v_ref[pl.ds(off, tk), :], _NT, preferred_element_type=F32)
            vt_ref[j, 0:A_DV, :] = vt.astype(BF16)
            vt_ref[j, A_DV:, :] = ones
            return carry
        lax.fori_loop(0, nchunk, tr, 0)
        vmt = lax.dot_general(eye, vm_ref[...], _NT, preferred_element_type=F32)
        vmt_ref[0:A_DV, :] = vmt.astype(BF16)
        vmt_ref[A_DV:, :] = ones[:, :PREFIX]

        row = lax.broadcasted_iota(jnp.int32, (tk, 128), 0)
        lane = lax.broadcasted_iota(jnp.int32, (tk, 128), 1)
        lo = row % 256
        kpos = jnp.where(lane < 3, lo, jnp.where(lane < 6, row - lo, 0))
        kpos_ref[...] = kpos.astype(F32).astype(BF16)
        arow = lax.broadcasted_iota(jnp.int32, (128, 2 * tq), 0)
        piece = jnp.where(arow % 3 == 0, slope_ref[h, 1], jnp.where(arow % 3 == 1, slope_ref[h, 2], slope_ref[h, 3]))
        qz_ref[128:256, :] = jnp.where(arow < 6, piece, 0.0).astype(BF16)

    qt = lax.dot_general(eye, q_ref[...], _NT, preferred_element_type=F32).astype(BF16)
    drow = lax.broadcasted_iota(jnp.int32, qt.shape, 0)
    zero = jnp.zeros_like(qt)
    qz_ref[0:128, :tq] = jnp.where(drow < A_DK, qt, zero)
    qz_ref[0:128, tq:] = jnp.where(drow >= A_DK, qt, zero)

    m_ref[...] = jnp.full(m_ref.shape, NEG, F32)
    acc_ref[...] = jnp.zeros(acc_ref.shape, F32)

    def scores(k, nk):
        lhs = jnp.concatenate([k, kpos_ref[0:nk, :]], axis=1)
        return jnp.dot(lhs, qz_ref[...], preferred_element_type=F32)

    def qk(kc, s_buf, cmax_ref):
        off = pl.multiple_of(kc * tk, tk)
        s = scores(k_ref[pl.ds(off, tk), :], tk)
        s_buf[...] = s
        cmax_ref[...] = jnp.max(s, axis=0, keepdims=True)

    def update(s, cmax, vt, c0):
        m_old = m_ref[...]
        m_new = jnp.maximum(m_old, cmax + c0)
        p = jnp.exp2(s - (m_new - c0)).astype(BF16)
        alpha = jnp.exp2(m_old - m_new)
        acc_ref[...] = alpha * acc_ref[...] + jnp.dot(vt, p, preferred_element_type=F32)
        m_ref[...] = m_new

    def masked_update(s, keep, vt, c0):
        s = jnp.where(jnp.concatenate([keep, keep], axis=1), s, NEG)
        update(s, jnp.max(s, axis=0, keepdims=True), vt, c0)

    def chunk_offset(kc):
        return -slope * ((qi - kc) * tk).astype(F32)

    s_prefix = scores(km_ref[...], PREFIX)

    qk(0, sa_ref, cmax_a)

    krow = lax.broadcasted_iota(jnp.int32, (PREFIX, tq), 0)
    masked_update(s_prefix, krow >= N_PAD, vmt_ref[...], -slope * ((qi * tq).astype(F32) + PREFIX))

    def pair(j, carry):
        kc = 2 * j
        qk(kc + 1, sb_ref, cmax_b)
        update(sa_ref[...], cmax_a[...], vt_ref[kc], chunk_offset(kc))
        qk(kc + 2, sa_ref, cmax_a)
        update(sb_ref[...], cmax_b[...], vt_ref[kc + 1], chunk_offset(kc + 1))
        return carry

    lax.fori_loop(0, qi // 2, pair, 0)

    kk = lax.broadcasted_iota(jnp.int32, (tk, tq), 0)
    qq = lax.broadcasted_iota(jnp.int32, (tk, tq), 1)
    causal = kk <= qq

    @pl.when(qi % 2 == 1)
    def _():
        qk(qi, sb_ref, cmax_b)
        update(sa_ref[...], cmax_a[...], vt_ref[qi - 1], chunk_offset(qi - 1))
        masked_update(sb_ref[...], causal, vt_ref[qi], 0.0)

    @pl.when(qi % 2 == 0)
    def _():
        masked_update(sa_ref[...], causal, vt_ref[qi], 0.0)

    lam =(jnp.exp(jnp.sum(lq1_ref[...] * lk1_ref[...], axis=-1, keepdims=True))
           - jnp.exp(jnp.sum(lq2_ref[...] * lk2_ref[...], axis=-1, keepdims=True)) + LAM_INIT)
    acc = acc_ref[...]
    o = acc[0:A_DV, :] / acc[A_DV:A_DV + 1, :]
    oa = (o[:, :tq] - lam * o[:, tq:]).T
    ms = jnp.mean(oa * oa, axis=-1, keepdims=True)
    oa = oa * lax.rsqrt(ms + EPS) * sg_ref[...] * (1.0 - LAM_INIT)
    g = g_ref[...]
    o_ref[...] = (oa * (g * jax.nn.sigmoid(g))).astype(o_ref.dtype)


def _slope_table():
    slope2 = (LOG2E * 2.0 ** (-8.0 * np.arange(1, A_HEADS + 1) / A_HEADS)).astype(np.float32)

    def top16(v):
        return (v.astype(np.float32).view(np.uint32) & np.uint32(0xFFFF0000)).view(np.float32)

    b1 = top16(slope2)
    b2 = top16(slope2 - b1)
    b3 = top16(slope2 - b1 - b2)
    return np.stack([slope2, b1, b2, b3], axis=1)


def _attention(slopes, lq1, lk1, lq2, lk2, subln_g, qkv, kv0, rest, tq):
    b, s, _ = qkv.shape
    tq = min(tq, s)
    tk = tq
    nh = A_HEADS
    small = lambda w: pl.BlockSpec((1, w), lambda bi, hi, qi: (0, 0))
    return pl.pallas_call(
        functools.partial(_attn_kernel, tq=tq, tk=tk),
        grid=(b, nh, s // tq),
        in_specs=[
            pl.BlockSpec(memory_space=pltpu.SMEM),
            small(A_DK), small(A_DK), small(A_DK), small(A_DK), small(A_DV),
            pl.BlockSpec((None, tq, 128), lambda bi, hi, qi: (bi, qi, hi)),
            pl.BlockSpec((None, s, 128), lambda bi, hi, qi: (bi, 0, nh + hi)),
            pl.BlockSpec((None, s, 128), lambda bi, hi, qi: (bi, 0, 2 * nh + hi)),
            pl.BlockSpec((PREFIX, 128), lambda bi, hi, qi: (0, hi)),
            pl.BlockSpec((PREFIX, 128), lambda bi, hi, qi: (0, nh + hi)),
            pl.BlockSpec((None, tq, 128), lambda bi, hi, qi: (bi, qi, OFF_G // 128 + hi)),
        ],
        out_specs=pl.BlockSpec((None, tq, 128), lambda bi, hi, qi: (bi, qi, hi)),
        out_shape=jax.ShapeDtypeStruct((b, s, V_W), BF16),
        scratch_shapes=[pltpu.VMEM((s // tk, A_DV + ONES_ROWS, tk), BF16),
                        pltpu.VMEM((A_DV + ONES_ROWS, PREFIX), BF16),
                        pltpu.VMEM((tk, 128), BF16),
                        pltpu.VMEM((256, 2 * tq), BF16),
                        pltpu.VMEM((tk, 2 * tq), F32),
                        pltpu.VMEM((tk, 2 * tq), F32),
                        pltpu.VMEM((1, 2 * tq), F32),
                        pltpu.VMEM((1, 2 * tq), F32),
                        pltpu.VMEM((1, 2 * tq), F32),
                        pltpu.VMEM((A_DV + ONES_ROWS, 2 * tq), F32)],
        compiler_params=_cparams(("parallel", "parallel", "arbitrary")),
        name="diff_attention",
    )(slopes, lq1, lk1, lq2, lk2, subln_g, qkv, qkv, qkv, kv0, kv0, rest)


def _sigmoid(x):
    return 0.5 * jnp.tanh(0.5 * x) + 0.5


def _silu(x):
    return x * _sigmoid(x)


def _split3(x):
    def top16(v):
        bits = lax.bitcast_convert_type(v, jnp.uint32) & jnp.uint32(0xFFFF0000)
        return lax.bitcast_convert_type(bits, F32)
    p1 = top16(x)
    r1 = x - p1
    p2 = top16(r1)
    return p1.astype(BF16), p2.astype(BF16), (r1 - p2).astype(BF16)


def _ssd_kernel(z_ref, xbc_ref, dt_ref, cw_ref, cb_ref, dtb_ref, alog_ref, dskip_ref, ng_ref,
                exp_ref, st0_ref, tail0_ref,
                y_ref, stout_ref, tailout_ref,
                hist_ref, state_ref, *, q, n_pad):
    c = pl.program_id(1)

    @pl.when(c == 0)
    def _():
        hist_ref[0:8, :] = tail0_ref[...]
        state_ref[...] = st0_ref[...]

    hist_ref[8:8 + q, :] = xbc_ref[...]
    conv = cb_ref[...]
    for j in range(S_CONV):
        conv = conv + cw_ref[j:j + 1, :] * hist_ref[8 - (S_CONV - 1) + j:8 - (S_CONV - 1) + j + q, :]
    hist_ref[0:8, :] = hist_ref[q:q + 8, :]
    xbc = _silu(conv)

    dt_raw = dt_ref[:, :S_HEADS] + dtb_ref[...]
    dt = jnp.maximum(dt_raw, 0.0) + jnp.log(1.0 + jnp.exp(-jnp.abs(dt_raw)))
    if n_pad:
        row = lax.broadcasted_iota(jnp.int32, (q, 1), 0)
        valid = row >= n_pad
        xbc = jnp.where(valid, xbc, 0.0)
        dt = jnp.where(valid, dt, 0.0)
    a = -jnp.exp(alog_ref[...])
    da = dt * a

    r_i = lax.broadcasted_iota(jnp.int32, (q, q), 0)
    c_i = lax.broadcasted_iota(jnp.int32, (q, q), 1)
    tril = c_i <= r_i
    acs = jnp.dot(tril.astype(F32), da, preferred_element_type=F32, precision=lax.Precision.HIGHEST)
    acs_t = acs.T

    pieces = _split3(jnp.concatenate([dt, acs], axis=0))
    wide = jnp.dot(jnp.concatenate(pieces, axis=1), exp_ref[...], preferred_element_type=F32)
    dt_w = wide[:q]
    acs_w = wide[q:]
    acs_last = acs_w[q - 1:q, :]

    xs = xbc[:, :D_SSM]
    bm = xbc[:, D_SSM:D_SSM + BC_W].astype(BF16)
    cm = xbc[:, D_SSM + BC_W:].astype(BF16)
    xdt = xs * dt_w
    xdec_b = (xdt * jnp.exp(acs_last - acs_w)).astype(BF16)
    state_old = state_ref[...]
    state_b = state_old.astype(BF16)
    lane = lax.broadcasted_iota(jnp.int32, (q, 128), 1)

    y_groups = []
    st_parts = []
    hpg = S_HEADS // S_GROUPS
    for g in range(S_GROUPS):
        bg = bm[:, g * S_STATE:(g + 1) * S_STATE]
        cg = cm[:, g * S_STATE:(g + 1) * S_STATE]
        cbm = lax.dot_general(cg, bg, (((1,), (1,)), ((), ())), preferred_element_type=F32)
        gs = slice(g * GROUP_W, (g + 1) * GROUP_W)
        y_off = jnp.dot(cg, state_b[:, gs], preferred_element_type=F32)
        st_parts.append(lax.dot_general(bg, xdec_b[:, gs], (((0,), (0,)), ((), ())),
                                        preferred_element_type=F32))
        pairs = []
        for pair in range(hpg // 2):
            h0 = g * hpg + 2 * pair
            cols = slice(h0 * S_HEADDIM, (h0 + 2) * S_HEADDIM)
            xp = xdt[:, cols].astype(BF16)
            ys = []
            for hh in (h0, h0 + 1):
                seg = acs[:, hh:hh + 1] - acs_t[hh:hh + 1, :]
                lmat = jnp.exp(jnp.where(tril, seg, -jnp.inf))
                ys.append(jnp.dot((cbm * lmat).astype(BF16), xp, preferred_element_type=F32))
            pairs.append(jnp.where(lane < S_HEADDIM, ys[0], ys[1]))
        y_groups.append(jnp.concatenate(pairs, axis=1) + y_off * jnp.exp(acs_w[:, gs]))

    y = jnp.concatenate(y_groups, axis=1)
    state_new = jnp.exp(acs_last) * state_old + jnp.concatenate(st_parts, axis=1)
    state_ref[...] = state_new

    y = y + xs * dskip_ref[...]
    z = z_ref[...]
    yg = y * _silu(z)
    outs = []
    for g in range(S_GROUPS):
        blk = yg[:, g * GROUP_W:(g + 1) * GROUP_W]
        ms = jnp.mean(blk * blk, axis=-1, keepdims=True)
        outs.append(blk * lax.rsqrt(ms + EPS))
    y_ref[...] = (jnp.concatenate(outs, axis=1) * ng_ref[...]).astype(y_ref.dtype)

    @pl.when(c == pl.num_programs(1) - 1)
    def _():
        stout_ref[...] = state_new
        tailout_ref[...] = hist_ref[0:8, :]


def _ssd(rest, conv_w, conv_b, dt_bias, a_log, dskip_w, norm_g, expand, state0, tail0, q, n_pad):
    b, s, _ = rest.shape
    q = min(q, s)
    const = lambda shp: pl.BlockSpec(shp, lambda bi, ci: (0,) * len(shp))
    return pl.pallas_call(
        functools.partial(_ssd_kernel, q=q, n_pad=n_pad),
        grid=(b, s // q),
        in_specs=[
            pl.BlockSpec((None, q, D_SSM), lambda bi, ci: (bi, ci, OFF_Z // D_SSM)),
            pl.BlockSpec((None, q, CONV_DIM), lambda bi, ci: (bi, ci, OFF_XBC // CONV_DIM)),
            pl.BlockSpec((None, q, 128), lambda bi, ci: (bi, ci, OFF_DT // 128)),
            const((S_CONV, CONV_DIM)), const((1, CONV_DIM)), const((1, S_HEADS)), const((1, S_HEADS)),
            const((1, D_SSM)), const((1, D_SSM)), const((3 * S_HEADS, D_SSM)),
            const((S_STATE, D_SSM)), const((8, CONV_DIM)),
        ],
        out_specs=[
            pl.BlockSpec((None, q, D_SSM), lambda bi, ci: (bi, ci, 0)),
            pl.BlockSpec((None, S_STATE, D_SSM), lambda bi, ci: (bi, 0, 0)),
            pl.BlockSpec((None, 8, CONV_DIM), lambda bi, ci: (bi, 0, 0)),
        ],
        out_shape=[
            jax.ShapeDtypeStruct((b, s, D_SSM), BF16),
            jax.ShapeDtypeStruct((b, S_STATE, D_SSM), F32),
            jax.ShapeDtypeStruct((b, 8, CONV_DIM), F32),
        ],
        scratch_shapes=[pltpu.VMEM((8 + q, CONV_DIM), F32), pltpu.VMEM((S_STATE, D_SSM), F32)],
        compiler_params=_cparams(("parallel", "arbitrary")),
        name="ssd",
    )(rest, rest, rest, conv_w, conv_b, dt_bias, a_log, dskip_w, norm_g, expand, state0, tail0)


def _out_kernel(x_ref, oa_ref, yg_ref, ma_ref, ms_ref, wa_ref, ws_ref, wo_ref, fg_ref, o_ref):
    y_att = jnp.dot(oa_ref[...], wa_ref[...], preferred_element_type=F32)
    y_ssm = jnp.dot(yg_ref[...], ws_ref[...], preferred_element_type=F32)
    merged = jax.nn.sigmoid(ma_ref[...]) * y_att + jax.nn.sigmoid(ms_ref[...]) * y_ssm
    hres = x_ref[...] + jnp.dot(merged.astype(BF16), wo_ref[...], preferred_element_type=F32)
    ms = jnp.mean(hres * hres, axis=-1, keepdims=True)
    o_ref[...] = hres * lax.rsqrt(ms + EPS) * fg_ref[...]


def _out(x, oa, yg, rest, w_a, w_s, w_o, final_g, tm):
    m, d = x.shape
    tm = min(tm, m)
    once = pl.Buffered(1)
    return pl.pallas_call(
        _out_kernel,
        grid=(m // tm,),
        in_specs=[
            pl.BlockSpec((tm, d), lambda i: (i, 0)),
            pl.BlockSpec((tm, V_W), lambda i: (i, 0)),
            pl.BlockSpec((tm, D_SSM), lambda i: (i, 0)),
            pl.BlockSpec((tm, d), lambda i: (i, OFF_MATT // D_MODEL)),
            pl.BlockSpec((tm, d), lambda i: (i, OFF_MSSM // D_MODEL)),
            pl.BlockSpec((V_W, d), lambda i: (0, 0), pipeline_mode=once),
            pl.BlockSpec((D_SSM, d), lambda i: (0, 0), pipeline_mode=once),
            pl.BlockSpec((d, d), lambda i: (0, 0), pipeline_mode=once),
            pl.BlockSpec((1, d), lambda i: (0, 0)),
        ],
        out_specs=pl.BlockSpec((tm, d), lambda i: (i, 0)),
        out_shape=jax.ShapeDtypeStruct((m, d), F32),
        compiler_params=_cparams(("parallel",)),
        name="out_proj",
    )(x, oa, yg, rest, rest, w_a, w_s, w_o, final_g)


def kernel(x, meta, norm_g, w_in, conv_w, conv_b, dt_bias, a_log, d_skip, ssm_norm_g,
           lam_q1, lam_k1, lam_q2, lam_k2, subln_g, w_br_attn, w_br_ssm, w_out, final_g):
    b, s, d = x.shape
    assert d == D_MODEL and norm_g.shape[0] == 1 and s % 128 == 0

    w = jnp.swapaxes(w_in[0], 0, 1)
    o_q, o_k, o_v, o_g = 0, Q_W, 2 * Q_W, 2 * Q_W + V_W
    o_z = o_g + V_W
    o_xbc = o_z + D_SSM
    o_dt = o_xbc + CONV_DIM
    o_ma = o_dt + S_HEADS
    o_ms = o_ma + D_MODEL
    w_qkv = jnp.concatenate([w[o_q:o_k] * (A_DK ** -0.5 * LOG2E), w[o_k:o_g]], axis=0).astype(BF16)
    w_rest = jnp.concatenate([
        w[o_z:o_xbc], w[o_ma:o_ms], w[o_ms:o_ms + D_MODEL], w[o_xbc:o_dt], w[o_g:o_z],
        w[o_dt:o_ma], jnp.zeros((DT_W - S_HEADS, d), w.dtype)], axis=0).astype(BF16)
    w_kv = w_qkv[Q_W:]

    g_in = norm_g[0][None, :]
    xf = x.reshape(b * s, d)
    h0 = jnp.concatenate([jnp.zeros((N_PAD, d), x.dtype), meta.astype(x.dtype)], axis=0)

    qkv = _inproj(xf, g_in, w_qkv, BF16, 1024, 1536).reshape(b, s, 3 * Q_W)
    rest = _inproj(xf, g_in, w_rest, F32, 1024, 1536).reshape(b, s, REST_W)
    kv0 = _inproj(h0, g_in, w_kv, BF16, PREFIX, 1024)
    rest0 = _inproj(h0, g_in, w_rest, F32, PREFIX, 1536).reshape(1, PREFIX, REST_W)

    slopes = jnp.asarray(_slope_table())
    row = lambda v: v.astype(F32).reshape(1, -1)
    oa = _attention(slopes, row(lam_q1[0]), row(lam_k1[0]), row(lam_q2[0]), row(lam_k2[0]), row(subln_g[0]),
                    qkv, kv0, rest, 512)

    dskip_w = jnp.repeat(d_skip[0].astype(F32), S_HEADDIM)[None, :]
    expand = jnp.tile(jnp.repeat(jnp.eye(S_HEADS, dtype=BF16), S_HEADDIM, axis=1), (3, 1))
    ssd_args = (conv_w[0].astype(F32), row(conv_b[0]), row(dt_bias[0]), row(a_log[0]), dskip_w,
                row(ssm_norm_g[0]), expand)
    zero_state = jnp.zeros((S_STATE, D_SSM), F32)
    zero_tail = jnp.zeros((8, CONV_DIM), F32)
    _, state0, tail0 = _ssd(rest0, *ssd_args, zero_state, zero_tail, PREFIX, N_PAD)
    yg, _, _ = _ssd(rest, *ssd_args, state0[0], tail0[0], 128, 0)

    out = _out(xf, oa.reshape(b * s, V_W), yg.reshape(b * s, D_SSM), rest.reshape(b * s, REST_W),
               w_br_attn[0].astype(BF16), w_br_ssm[0].astype(BF16), w_out[0].astype(BF16),
               row(final_g), 256)
    return out.reshape(b, s, d)
```

```python
import functools
import math

import jax
import jax.numpy as jnp
import numpy as np
from jax import lax
from jax.experimental import pallas as pl
from jax.experimental.pallas import tpu as pltpu

F32 = jnp.float32
BF16 = jnp.bfloat16

D_MODEL = 2048
N_META = 16
PREFIX = 128
N_PAD = PREFIX - N_META
A_HEADS = 8
A_DK = 64
A_DV = 128
Q_W = A_HEADS * 2 * A_DK
V_W = A_HEADS * A_DV
D_SSM = 2048
S_HEADDIM = 64
S_HEADS = D_SSM // S_HEADDIM
S_GROUPS = 4
S_STATE = 128
S_CONV = 4
GROUP_W = D_SSM // S_GROUPS
BC_W = S_GROUPS * S_STATE
CONV_DIM = D_SSM + 2 * BC_W
DT_W = 512
EPS = 1e-6
NEG = -1e30
LAM_INIT = 0.8 - 0.6 * math.exp(-0.3 * 0)

OFF_Z = 0
OFF_MATT = OFF_Z + D_SSM
OFF_MSSM = OFF_MATT + D_MODEL
OFF_XBC = OFF_MSSM + D_MODEL
OFF_G = OFF_XBC + CONV_DIM
OFF_DT = OFF_G + V_W
REST_W = OFF_DT + DT_W

VMEM_LIMIT = 56 * 1024 * 1024


def _cparams(sem):
    return pltpu.CompilerParams(dimension_semantics=sem, vmem_limit_bytes=VMEM_LIMIT)


def _inproj_kernel(x_ref, g_ref, w_ref, o_ref, u_ref):
    @pl.when(pl.program_id(1) == 0)
    def _():
        x = x_ref[...]
        ms = jnp.mean(x * x, axis=-1, keepdims=True)
        u_ref[...] = (x * lax.rsqrt(ms + EPS) * g_ref[...]).astype(BF16)

    o_ref[...] = lax.dot_general(u_ref[...], w_ref[...], _NT, preferred_element_type=F32).astype(o_ref.dtype)


def _inproj(x, g, w, out_dtype, tm, tn):
    m, d = x.shape
    n = w.shape[0]
    tm = min(tm, m)
    return pl.pallas_call(
        _inproj_kernel,
        grid=(m // tm, n // tn),
        in_specs=[
            pl.BlockSpec((tm, d), lambda i, j: (i, 0)),
            pl.BlockSpec((1, d), lambda i, j: (0, 0)),
            pl.BlockSpec((tn, d), lambda i, j: (j, 0)),
        ],
        out_specs=pl.BlockSpec((tm, tn), lambda i, j: (i, j)),
        out_shape=jax.ShapeDtypeStruct((m, n), out_dtype),
        scratch_shapes=[pltpu.VMEM((tm, d), BF16)],
        compiler_params=_cparams(("parallel", "arbitrary")),
        name="in_proj",
    )(x, g, w)


ONES_ROWS = 16
LOG2E = math.log2(math.e)
_NT = (((1,), (1,)), ((), ()))


def _attn_kernel(slope_ref, lq1_ref, lk1_ref, lq2_ref, lk2_ref, sg_ref,
                 q_ref, k_ref, v_ref, km_ref, vm_ref, g_ref, o_ref,
                 vt_ref, vmt_ref, kpos_ref, qz_ref, sa_ref, sb_ref, cmax_a, cmax_b, m_ref, acc_ref,
                 *, tq, tk):
    h = pl.program_id(1)
    qi = pl.program_id(2)
    slope = slope_ref[h, 0]
    nchunk = k_ref.shape[0] // tk
    r_e = lax.broadcasted_iota(jnp.int32, (128, 128), 0)
    c_e = lax.broadcasted_iota(jnp.int32, (128, 128), 1)
    eye = (r_e == c_e).astype(BF16)

    @pl.when(qi == 0)
    def _():
        ones = jnp.ones((ONES_ROWS, tk), BF16)

        def tr(j, carry):
            off = pl.multiple_of(j * tk, tk)
            vt = lax.dot_general(eye, v_ref[pl.ds(off, tk), :], _NT, preferred_element_type=F32)
            vt_ref[j, 0:A_DV, :] = vt.astype(BF16)
            vt_ref[j, A_DV:, :] = ones
            return carry
        lax.fori_loop(0, nchunk, tr, 0)
        vmt = lax.dot_general(eye, vm_ref[...], _NT, preferred_element_type=F32)
        vmt_ref[0:A_DV, :] = vmt.astype(BF16)
        vmt_ref[A_DV:, :] = ones[:, :PREFIX]

        row = lax.broadcasted_iota(jnp.int32, (tk, 128), 0)
        lane = lax.broadcasted_iota(jnp.int32, (tk, 128), 1)
        lo = row % 256
        kpos = jnp.where(lane < 3, lo, jnp.where(lane < 6, row - lo, 0))
        kpos_ref[...] = kpos.astype(F32).astype(BF16)
        arow = lax.broadcasted_iota(jnp.int32, (128, 2 * tq), 0)
        piece = jnp.where(arow % 3 == 0, slope_ref[h, 1], jnp.where(arow % 3 == 1, slope_ref[h, 2], slope_ref[h, 3]))
        qz_ref[128:256, :] = jnp.where(arow < 6, piece, 0.0).astype(BF16)

    qt = lax.dot_general(eye, q_ref[...], _NT, preferred_element_type=F32).astype(BF16)
    drow = lax.broadcasted_iota(jnp.int32, qt.shape, 0)
    zero = jnp.zeros_like(qt)
    qz_ref[0:128, :tq] = jnp.where(drow < A_DK, qt, zero)
    qz_ref[0:128, tq:] = jnp.where(drow >= A_DK, qt, zero)

    m_ref[...] = jnp.full(m_ref.shape, NEG, F32)
    acc_ref[...] = jnp.zeros(acc_ref.shape, F32)

    nstrip = 2 * tq // tk
    strip = lambda i: slice(i * tk, (i + 1) * tk)

    def scores(k, nk, i):
        lhs = jnp.concatenate([k, kpos_ref[0:nk, :]], axis=1)
        return jnp.dot(lhs, qz_ref[:, strip(i)], preferred_element_type=F32)

    def qk(kc, i, s_buf, cmax_ref):
        off = pl.multiple_of(kc * tk, tk)
        s = scores(k_ref[pl.ds(off, tk), :], tk, i)
        s_buf[:, strip(i)] = s
        cmax_ref[:, strip(i)] = jnp.max(s, axis=0, keepdims=True)

    def update(i, s, cmax, vt, c0):
        m_old = m_ref[:, strip(i)]
        m_new = jnp.maximum(m_old, cmax + c0)
        p = jnp.exp2(s - (m_new - c0)).astype(BF16)
        alpha = jnp.exp2(m_old - m_new)
        acc_ref[:, strip(i)] = alpha * acc_ref[:, strip(i)] + jnp.dot(vt, p, preferred_element_type=F32)
        m_ref[:, strip(i)] = m_new

    def masked_update(i, s, keep, vt, c0):
        s = jnp.where(keep, s, NEG)
        update(i, s, jnp.max(s, axis=0, keepdims=True), vt, c0)

    def chunk_offset(kc):
        return -slope * (qi * tq - kc * tk).astype(F32)

    krow = lax.broadcasted_iota(jnp.int32, (PREFIX, tk), 0)
    for i in range(nstrip):
        qk(0, i, sa_ref, cmax_a)
        masked_update(i, scores(km_ref[...], PREFIX, i), krow >= N_PAD, vmt_ref[...],
                      -slope * ((qi * tq).astype(F32) + PREFIX))

    def pair(j):
        kc = 2 * j
        for i in range(nstrip):
            qk(kc + 1, i, sb_ref, cmax_b)
            update(i, sa_ref[:, strip(i)], cmax_a[:, strip(i)], vt_ref[kc], chunk_offset(kc))
        for i in range(nstrip):
            qk(kc + 2, i, sa_ref, cmax_a)
            update(i, sb_ref[:, strip(i)], cmax_b[:, strip(i)], vt_ref[kc + 1], chunk_offset(kc + 1))

    def two_pairs(t, carry):
        pair(2 * t)
        pair(2 * t + 1)
        return carry

    lax.fori_loop(0, qi // 2, two_pairs, 0)

    @pl.when(qi % 2 == 1)
    def _():
        pair(qi - 1)

    kk = lax.broadcasted_iota(jnp.int32, (tk, tk), 0)
    qq = lax.broadcasted_iota(jnp.int32, (tk, tk), 1)
    causal = kk <= qq
    kd = 2 * qi
    for i in range(nstrip):
        if i % 2 == 0:
            masked_update(i, sa_ref[:, strip(i)], causal, vt_ref[kd], chunk_offset(kd))
        else:
            qk(kd + 1, i, sb_ref, cmax_b)
            update(i, sa_ref[:, strip(i)], cmax_a[:, strip(i)], vt_ref[kd], chunk_offset(kd))
    for i in range(1, nstrip, 2):
        masked_update(i, sb_ref[:, strip(i)], causal, vt_ref[kd + 1], chunk_offset(kd + 1))

    lam =(jnp.exp(jnp.sum(lq1_ref[...] * lk1_ref[...], axis=-1, keepdims=True))
           - jnp.exp(jnp.sum(lq2_ref[...] * lk2_ref[...], axis=-1, keepdims=True)) + LAM_INIT)
    acc = acc_ref[...]
    o = acc[0:A_DV, :] / acc[A_DV:A_DV + 1, :]
    oa = (o[:, :tq] - lam * o[:, tq:]).T
    ms = jnp.mean(oa * oa, axis=-1, keepdims=True)
    oa = oa * lax.rsqrt(ms + EPS) * sg_ref[...] * (1.0 - LAM_INIT)
    g = g_ref[...]
    o_ref[...] = (oa * (g * jax.nn.sigmoid(g))).astype(o_ref.dtype)


def _slope_table():
    slope2 = (LOG2E * 2.0 ** (-8.0 * np.arange(1, A_HEADS + 1) / A_HEADS)).astype(np.float32)

    def top16(v):
        return (v.astype(np.float32).view(np.uint32) & np.uint32(0xFFFF0000)).view(np.float32)

    b1 = top16(slope2)
    b2 = top16(slope2 - b1)
    b3 = top16(slope2 - b1 - b2)
    return np.stack([slope2, b1, b2, b3], axis=1)


def _attention(slopes, lq1, lk1, lq2, lk2, subln_g, qkv, kv0, rest, tq):
    b, s, _ = qkv.shape
    assert s % tq == 0 and tq % 2 == 0
    tk = tq // 2
    nh = A_HEADS
    small = lambda w: pl.BlockSpec((1, w), lambda bi, hi, qi: (0, 0))
    return pl.pallas_call(
        functools.partial(_attn_kernel, tq=tq, tk=tk),
        grid=(b, nh, s // tq),
        in_specs=[
            pl.BlockSpec(memory_space=pltpu.SMEM),
            small(A_DK), small(A_DK), small(A_DK), small(A_DK), small(A_DV),
            pl.BlockSpec((None, tq, 128), lambda bi, hi, qi: (bi, qi, hi)),
            pl.BlockSpec((None, s, 128), lambda bi, hi, qi: (bi, 0, nh + hi)),
            pl.BlockSpec((None, s, 128), lambda bi, hi, qi: (bi, 0, 2 * nh + hi)),
            pl.BlockSpec((PREFIX, 128), lambda bi, hi, qi: (0, hi)),
            pl.BlockSpec((PREFIX, 128), lambda bi, hi, qi: (0, nh + hi)),
            pl.BlockSpec((None, tq, 128), lambda bi, hi, qi: (bi, qi, OFF_G // 128 + hi)),
        ],
        out_specs=pl.BlockSpec((None, tq, 128), lambda bi, hi, qi: (bi, qi, hi)),
        out_shape=jax.ShapeDtypeStruct((b, s, V_W), BF16),
        scratch_shapes=[pltpu.VMEM((s // tk, A_DV + ONES_ROWS, tk), BF16),
                        pltpu.VMEM((A_DV + ONES_ROWS, PREFIX), BF16),
                        pltpu.VMEM((tk, 128), BF16),
                        pltpu.VMEM((256, 2 * tq), BF16),
                        pltpu.VMEM((tk, 2 * tq), F32),
                        pltpu.VMEM((tk, 2 * tq), F32),
                        pltpu.VMEM((1, 2 * tq), F32),
                        pltpu.VMEM((1, 2 * tq), F32),
                        pltpu.VMEM((1, 2 * tq), F32),
                        pltpu.VMEM((A_DV + ONES_ROWS, 2 * tq), F32)],
        compiler_params=_cparams(("parallel", "parallel", "arbitrary")),
        name="diff_attention",
    )(slopes, lq1, lk1, lq2, lk2, subln_g, qkv, qkv, qkv, kv0, kv0, rest)


def _sigmoid(x):
    return 0.5 * jnp.tanh(0.5 * x) + 0.5


def _silu(x):
    return x * _sigmoid(x)


def _split3(x):
    def top16(v):
        bits = lax.bitcast_convert_type(v, jnp.uint32) & jnp.uint32(0xFFFF0000)
        return lax.bitcast_convert_type(bits, F32)
    p1 = top16(x)
    r1 = x - p1
    p2 = top16(r1)
    return p1.astype(BF16), p2.astype(BF16), (r1 - p2).astype(BF16)


def _ssd_kernel(z_ref, xbc_ref, dt_ref, cw_ref, cb_ref, dtb_ref, alog_ref, dskip_ref, ng_ref,
                exp_ref, st0_ref, tail0_ref,
                y_ref, stout_ref, tailout_ref,
                hist_ref, state_ref, *, q, n_pad):
    c = pl.program_id(1)

    @pl.when(c == 0)
    def _():
        hist_ref[0:8, :] = tail0_ref[...]
        state_ref[...] = st0_ref[...]

    hist_ref[8:8 + q, :] = xbc_ref[...]
    conv = cb_ref[...]
    for j in range(S_CONV):
        conv = conv + cw_ref[j:j + 1, :] * hist_ref[8 - (S_CONV - 1) + j:8 - (S_CONV - 1) + j + q, :]
    hist_ref[0:8, :] = hist_ref[q:q + 8, :]
    xbc = _silu(conv)

    dt_raw = dt_ref[:, :S_HEADS] + dtb_ref[...]
    dt = jnp.maximum(dt_raw, 0.0) + jnp.log(1.0 + jnp.exp(-jnp.abs(dt_raw)))
    if n_pad:
        row = lax.broadcasted_iota(jnp.int32, (q, 1), 0)
        valid = row >= n_pad
        xbc = jnp.where(valid, xbc, 0.0)
        dt = jnp.where(valid, dt, 0.0)
    a = -jnp.exp(alog_ref[...])
    da = dt * a

    r_i = lax.broadcasted_iota(jnp.int32, (q, q), 0)
    c_i = lax.broadcasted_iota(jnp.int32, (q, q), 1)
    tril = c_i <= r_i
    acs = jnp.dot(tril.astype(F32), da, preferred_element_type=F32, precision=lax.Precision.HIGHEST)
    acs_t = acs.T

    pieces = _split3(jnp.concatenate([dt, acs], axis=0))
    wide = jnp.dot(jnp.concatenate(pieces, axis=1), exp_ref[...], preferred_element_type=F32)
    dt_w = wide[:q]
    acs_w = wide[q:]
    acs_last = acs_w[q - 1:q, :]

    xs = xbc[:, :D_SSM]
    bm = xbc[:, D_SSM:D_SSM + BC_W].astype(BF16)
    cm = xbc[:, D_SSM + BC_W:].astype(BF16)
    xdt = xs * dt_w
    xdec_b = (xdt * jnp.exp(acs_last - acs_w)).astype(BF16)
    state_old = state_ref[...]
    state_b = state_old.astype(BF16)
    lane = lax.broadcasted_iota(jnp.int32, (q, 128), 1)

    y_groups = []
    st_parts = []
    hpg = S_HEADS // S_GROUPS
    for g in range(S_GROUPS):
        bg = bm[:, g * S_STATE:(g + 1) * S_STATE]
        cg = cm[:, g * S_STATE:(g + 1) * S_STATE]
        cbm = lax.dot_general(cg, bg, (((1,), (1,)), ((), ())), preferred_element_type=F32)
        gs = slice(g * GROUP_W, (g + 1) * GROUP_W)
        y_off = jnp.dot(cg, state_b[:, gs], preferred_element_type=F32)
        st_parts.append(lax.dot_general(bg, xdec_b[:, gs], (((0,), (0,)), ((), ())),
                                        preferred_element_type=F32))
        pairs = []
        for pair in range(hpg // 2):
            h0 = g * hpg + 2 * pair
            cols = slice(h0 * S_HEADDIM, (h0 + 2) * S_HEADDIM)
            xp = xdt[:, cols].astype(BF16)
            ys = []
            for hh in (h0, h0 + 1):
                seg = acs[:, hh:hh + 1] - acs_t[hh:hh + 1, :]
                lmat = jnp.exp(jnp.where(tril, seg, -jnp.inf))
                ys.append(jnp.dot((cbm * lmat).astype(BF16), xp, preferred_element_type=F32))
            pairs.append(jnp.where(lane < S_HEADDIM, ys[0], ys[1]))
        y_groups.append(jnp.concatenate(pairs, axis=1) + y_off * jnp.exp(acs_w[:, gs]))

    y = jnp.concatenate(y_groups, axis=1)
    state_new = jnp.exp(acs_last) * state_old + jnp.concatenate(st_parts, axis=1)
    state_ref[...] = state_new

    y = y + xs * dskip_ref[...]
    z = z_ref[...]
    yg = y * _silu(z)
    outs = []
    for g in range(S_GROUPS):
        blk = yg[:, g * GROUP_W:(g + 1) * GROUP_W]
        ms = jnp.mean(blk * blk, axis=-1, keepdims=True)
        outs.append(blk * lax.rsqrt(ms + EPS))
    y_ref[...] = (jnp.concatenate(outs, axis=1) * ng_ref[...]).astype(y_ref.dtype)

    @pl.when(c == pl.num_programs(1) - 1)
    def _():
        stout_ref[...] = state_new
        tailout_ref[...] = hist_ref[0:8, :]


def _ssd(rest, conv_w, conv_b, dt_bias, a_log, dskip_w, norm_g, expand, state0, tail0, q, n_pad):
    b, s, _ = rest.shape
    q = min(q, s)
    const = lambda shp: pl.BlockSpec(shp, lambda bi, ci: (0,) * len(shp))
    return pl.pallas_call(
        functools.partial(_ssd_kernel, q=q, n_pad=n_pad),
        grid=(b, s // q),
        in_specs=[
            pl.BlockSpec((None, q, D_SSM), lambda bi, ci: (bi, ci, OFF_Z // D_SSM)),
            pl.BlockSpec((None, q, CONV_DIM), lambda bi, ci: (bi, ci, OFF_XBC // CONV_DIM)),
            pl.BlockSpec((None, q, 128), lambda bi, ci: (bi, ci, OFF_DT // 128)),
            const((S_CONV, CONV_DIM)), const((1, CONV_DIM)), const((1, S_HEADS)), const((1, S_HEADS)),
            const((1, D_SSM)), const((1, D_SSM)), const((3 * S_HEADS, D_SSM)),
            const((S_STATE, D_SSM)), const((8, CONV_DIM)),
        ],
        out_specs=[
            pl.BlockSpec((None, q, D_SSM), lambda bi, ci: (bi, ci, 0)),
            pl.BlockSpec((None, S_STATE, D_SSM), lambda bi, ci: (bi, 0, 0)),
            pl.BlockSpec((None, 8, CONV_DIM), lambda bi, ci: (bi, 0, 0)),
        ],
        out_shape=[
            jax.ShapeDtypeStruct((b, s, D_SSM), BF16),
            jax.ShapeDtypeStruct((b, S_STATE, D_SSM), F32),
            jax.ShapeDtypeStruct((b, 8, CONV_DIM), F32),
        ],
        scratch_shapes=[pltpu.VMEM((8 + q, CONV_DIM), F32), pltpu.VMEM((S_STATE, D_SSM), F32)],
        compiler_params=_cparams(("parallel", "arbitrary")),
        name="ssd",
    )(rest, rest, rest, conv_w, conv_b, dt_bias, a_log, dskip_w, norm_g, expand, state0, tail0)


def _out_kernel(x_ref, oa_ref, yg_ref, ma_ref, ms_ref, wa_ref, ws_ref, wo_ref, fg_ref, o_ref):
    y_att = jnp.dot(oa_ref[...], wa_ref[...], preferred_element_type=F32)
    y_ssm = jnp.dot(yg_ref[...], ws_ref[...], preferred_element_type=F32)
    merged = jax.nn.sigmoid(ma_ref[...]) * y_att + jax.nn.sigmoid(ms_ref[...]) * y_ssm
    hres = x_ref[...] + jnp.dot(merged.astype(BF16), wo_ref[...], preferred_element_type=F32)
    ms = jnp.mean(hres * hres, axis=-1, keepdims=True)
    o_ref[...] = hres * lax.rsqrt(ms + EPS) * fg_ref[...]


def _out(x, oa, yg, rest, w_a, w_s, w_o, final_g, tm):
    m, d = x.shape
    tm = min(tm, m)
    once = pl.Buffered(1)
    return pl.pallas_call(
        _out_kernel,
        grid=(m // tm,),
        in_specs=[
            pl.BlockSpec((tm, d), lambda i: (i, 0)),
            pl.BlockSpec((tm, V_W), lambda i: (i, 0)),
            pl.BlockSpec((tm, D_SSM), lambda i: (i, 0)),
            pl.BlockSpec((tm, d), lambda i: (i, OFF_MATT // D_MODEL)),
            pl.BlockSpec((tm, d), lambda i: (i, OFF_MSSM // D_MODEL)),
            pl.BlockSpec((V_W, d), lambda i: (0, 0), pipeline_mode=once),
            pl.BlockSpec((D_SSM, d), lambda i: (0, 0), pipeline_mode=once),
            pl.BlockSpec((d, d), lambda i: (0, 0), pipeline_mode=once),
            pl.BlockSpec((1, d), lambda i: (0, 0)),
        ],
        out_specs=pl.BlockSpec((tm, d), lambda i: (i, 0)),
        out_shape=jax.ShapeDtypeStruct((m, d), F32),
        compiler_params=_cparams(("parallel",)),
        name="out_proj",
    )(x, oa, yg, rest, rest, w_a, w_s, w_o, final_g)


def kernel(x, meta, norm_g, w_in, conv_w, conv_b, dt_bias, a_log, d_skip, ssm_norm_g,
           lam_q1, lam_k1, lam_q2, lam_k2, subln_g, w_br_attn, w_br_ssm, w_out, final_g):
    b, s, d = x.shape
    assert d == D_MODEL and norm_g.shape[0] == 1 and s % 128 == 0

    w = jnp.swapaxes(w_in[0], 0, 1)
    o_q, o_k, o_v, o_g = 0, Q_W, 2 * Q_W, 2 * Q_W + V_W
    o_z = o_g + V_W
    o_xbc = o_z + D_SSM
    o_dt = o_xbc + CONV_DIM
    o_ma = o_dt + S_HEADS
    o_ms = o_ma + D_MODEL
    w_qkv = jnp.concatenate([w[o_q:o_k] * (A_DK ** -0.5 * LOG2E), w[o_k:o_g]], axis=0).astype(BF16)
    w_rest = jnp.concatenate([
        w[o_z:o_xbc], w[o_ma:o_ms], w[o_ms:o_ms + D_MODEL], w[o_xbc:o_dt], w[o_g:o_z],
        w[o_dt:o_ma], jnp.zeros((DT_W - S_HEADS, d), w.dtype)], axis=0).astype(BF16)
    w_kv = w_qkv[Q_W:]

    g_in = norm_g[0][None, :]
    xf = x.reshape(b * s, d)
    h0 = jnp.concatenate([jnp.zeros((N_PAD, d), x.dtype), meta.astype(x.dtype)], axis=0)

    qkv = _inproj(xf, g_in, w_qkv, BF16, 1024, 1536).reshape(b, s, 3 * Q_W)
    rest = _inproj(xf, g_in, w_rest, F32, 1024, 1536).reshape(b, s, REST_W)
    kv0 = _inproj(h0, g_in, w_kv, BF16, PREFIX, 1024)
    rest0 = _inproj(h0, g_in, w_rest, F32, PREFIX, 1536).reshape(1, PREFIX, REST_W)

    slopes = jnp.asarray(_slope_table())
    row = lambda v: v.astype(F32).reshape(1, -1)
    oa = _attention(slopes, row(lam_q1[0]), row(lam_k1[0]), row(lam_q2[0]), row(lam_k2[0]), row(subln_g[0]),
                    qkv, kv0, rest, min(1024, s))

    dskip_w = jnp.repeat(d_skip[0].astype(F32), S_HEADDIM)[None, :]
    expand = jnp.tile(jnp.repeat(jnp.eye(S_HEADS, dtype=BF16), S_HEADDIM, axis=1), (3, 1))
    ssd_args = (conv_w[0].astype(F32), row(conv_b[0]), row(dt_bias[0]), row(a_log[0]), dskip_w,
                row(ssm_norm_g[0]), expand)
    zero_state = jnp.zeros((S_STATE, D_SSM), F32)
    zero_tail = jnp.zeros((8, CONV_DIM), F32)
    _, state0, tail0 = _ssd(rest0, *ssd_args, zero_state, zero_tail, PREFIX, N_PAD)
    yg, _, _ = _ssd(rest, *ssd_args, state0[0], tail0[0], 128, 0)

    out = _out(xf, oa.reshape(b * s, V_W), yg.reshape(b * s, D_SSM), rest.reshape(b * s, REST_W),
               w_br_attn[0].astype(BF16), w_br_ssm[0].astype(BF16), w_out[0].astype(BF16),
               row(final_g), 256)
    return out.reshape(b, s, d)
```

```python
import functools
import math

import jax
import jax.numpy as jnp
import numpy as np
from jax import lax
from jax.experimental import pallas as pl
from jax.experimental.pallas import tpu as pltpu

F32 = jnp.float32
BF16 = jnp.bfloat16

D_MODEL = 2048
N_META = 16
PREFIX = 128
N_PAD = PREFIX - N_META
A_HEADS = 8
A_DK = 64
A_DV = 128
Q_W = A_HEADS * 2 * A_DK
V_W = A_HEADS * A_DV
D_SSM = 2048
S_HEADDIM = 64
S_HEADS = D_SSM // S_HEADDIM
S_GROUPS = 4
S_STATE = 128
S_CONV = 4
GROUP_W = D_SSM // S_GROUPS
BC_W = S_GROUPS * S_STATE
CONV_DIM = D_SSM + 2 * BC_W
DT_W = 512
EPS = 1e-6
NEG = -1e30
LAM_INIT = 0.8 - 0.6 * math.exp(-0.3 * 0)

OFF_Z = 0
OFF_MATT = OFF_Z + D_SSM
OFF_MSSM = OFF_MATT + D_MODEL
OFF_XBC = OFF_MSSM + D_MODEL
OFF_G = OFF_XBC + CONV_DIM
OFF_DT = OFF_G + V_W
REST_W = OFF_DT + DT_W

VMEM_LIMIT = 56 * 1024 * 1024


def _cparams(sem):
    return pltpu.CompilerParams(dimension_semantics=sem, vmem_limit_bytes=VMEM_LIMIT)


def _inproj_kernel(x_ref, g_ref, w_ref, o_ref, u_ref):
    @pl.when(pl.program_id(1) == 0)
    def _():
        x = x_ref[...]
        ms = jnp.mean(x * x, axis=-1, keepdims=True)
        u_ref[...] = (x * lax.rsqrt(ms + EPS) * g_ref[...]).astype(BF16)

    o_ref[...] = lax.dot_general(u_ref[...], w_ref[...], _NT, preferred_element_type=F32).astype(o_ref.dtype)


def _inproj(x, g, w, out_dtype, tm, tn):
    m, d = x.shape
    n = w.shape[0]
    tm = min(tm, m)
    return pl.pallas_call(
        _inproj_kernel,
        grid=(m // tm, n // tn),
        in_specs=[
            pl.BlockSpec((tm, d), lambda i, j: (i, 0)),
            pl.BlockSpec((1, d), lambda i, j: (0, 0)),
            pl.BlockSpec((tn, d), lambda i, j: (j, 0)),
        ],
        out_specs=pl.BlockSpec((tm, tn), lambda i, j: (i, j)),
        out_shape=jax.ShapeDtypeStruct((m, n), out_dtype),
        scratch_shapes=[pltpu.VMEM((tm, d), BF16)],
        compiler_params=_cparams(("parallel", "arbitrary")),
        name="in_proj",
    )(x, g, w)


ONES_ROWS = 16
LOG2E = math.log2(math.e)
_NT = (((1,), (1,)), ((), ()))


def _attn_kernel(slope_ref, lq1_ref, lk1_ref, lq2_ref, lk2_ref, sg_ref,
                 q_ref, k_ref, v_ref, km_ref, vm_ref, g_ref, o_ref,
                 vt_ref, vmt_ref, kpos_ref, qz_ref, sa_ref, sb_ref, cmax_a, cmax_b, m_ref, acc_ref,
                 *, tq, tk):
    h = pl.program_id(1)
    qi = pl.program_id(2)
    slope = slope_ref[h, 0]
    nchunk = k_ref.shape[0] // tk
    r_e = lax.broadcasted_iota(jnp.int32, (128, 128), 0)
    c_e = lax.broadcasted_iota(jnp.int32, (128, 128), 1)
    eye = (r_e == c_e).astype(BF16)

    @pl.when(qi == 0)
    def _():
        ones = jnp.ones((ONES_ROWS, tk), BF16)

        def tr(j, carry):
            off = pl.multiple_of(j * tk, tk)
            vt = lax.dot_general(eye, v_ref[pl.ds(off, tk), :], _NT, preferred_element_type=F32)
            vt_ref[j, 0:A_DV, :] = vt.astype(BF16)
            vt_ref[j, A_DV:, :] = ones
            return carry
        lax.fori_loop(0, nchunk, tr, 0, unroll=4)
        vmt = lax.dot_general(eye, vm_ref[...], _NT, preferred_element_type=F32)
        vmt_ref[0:A_DV, :] = vmt.astype(BF16)
        vmt_ref[A_DV:, :] = ones[:, :PREFIX]

        row = lax.broadcasted_iota(jnp.int32, (tk, 128), 0)
        lane = lax.broadcasted_iota(jnp.int32, (tk, 128), 1)
        lo = row % 256
        kpos = jnp.where(lane < 3, lo, jnp.where(lane < 6, row - lo, 0))
        kpos_ref[...] = kpos.astype(F32).astype(BF16)
        arow = lax.broadcasted_iota(jnp.int32, (128, 2 * tq), 0)
        piece = jnp.where(arow % 3 == 0, slope_ref[h, 1], jnp.where(arow % 3 == 1, slope_ref[h, 2], slope_ref[h, 3]))
        qz_ref[128:256, :] = jnp.where(arow < 6, piece, 0.0).astype(BF16)

    qt = lax.dot_general(eye, q_ref[...], _NT, preferred_element_type=F32).astype(BF16)
    drow = lax.broadcasted_iota(jnp.int32, qt.shape, 0)
    zero = jnp.zeros_like(qt)
    qz_ref[0:128, :tq] = jnp.where(drow < A_DK, qt, zero)
    qz_ref[0:128, tq:] = jnp.where(drow >= A_DK, qt, zero)

    m_ref[...] = jnp.full(m_ref.shape, NEG, F32)
    acc_ref[...] = jnp.zeros(acc_ref.shape, F32)

    nstrip = 2 * tq // tk
    strip = lambda i: slice(i * tk, (i + 1) * tk)

    def scores(k, nk, i):
        lhs = jnp.concatenate([k, kpos_ref[0:nk, :]], axis=1)
        return jnp.dot(lhs, qz_ref[:, strip(i)], preferred_element_type=F32)

    def qk(kc, i, s_buf, cmax_ref):
        off = pl.multiple_of(kc * tk, tk)
        s = scores(k_ref[pl.ds(off, tk), :], tk, i)
        s_buf[:, strip(i)] = s
        cmax_ref[:, strip(i)] = jnp.max(s, axis=0, keepdims=True)

    def update(i, s, cmax, vt, c0):
        m_old = m_ref[:, strip(i)]
        m_new = jnp.maximum(m_old, cmax + c0)
        p = jnp.exp2(s - (m_new - c0)).astype(BF16)
        alpha = jnp.exp2(m_old - m_new)
        acc_ref[:, strip(i)] = alpha * acc_ref[:, strip(i)] + jnp.dot(vt, p, preferred_element_type=F32)
        m_ref[:, strip(i)] = m_new

    def masked_update(i, s, keep, vt, c0):
        s = jnp.where(keep, s, NEG)
        update(i, s, jnp.max(s, axis=0, keepdims=True), vt, c0)

    def chunk_offset(kc):
        return -slope * (qi * tq - kc * tk).astype(F32)

    s_prefix = [scores(km_ref[...], PREFIX, i) for i in range(nstrip)]
    for i in range(nstrip):
        qk(0, i, sa_ref, cmax_a)
    krow = lax.broadcasted_iota(jnp.int32, (PREFIX, tk), 0)
    for i in range(nstrip):
        masked_update(i, s_prefix[i], krow >= N_PAD, vmt_ref[...], -slope * ((qi * tq).astype(F32) + PREFIX))

    def pair(j):
        kc = 2 * j
        for i in range(nstrip):
            qk(kc + 1, i, sb_ref, cmax_b)
            update(i, sa_ref[:, strip(i)], cmax_a[:, strip(i)], vt_ref[kc], chunk_offset(kc))
        for i in range(nstrip):
            qk(kc + 2, i, sa_ref, cmax_a)
            update(i, sb_ref[:, strip(i)], cmax_b[:, strip(i)], vt_ref[kc + 1], chunk_offset(kc + 1))

    def two_pairs(t, carry):
        pair(2 * t)
        pair(2 * t + 1)
        return carry

    lax.fori_loop(0, qi // 2, two_pairs, 0)

    @pl.when(qi % 2 == 1)
    def _():
        pair(qi - 1)

    kk = lax.broadcasted_iota(jnp.int32, (tk, tk), 0)
    qq = lax.broadcasted_iota(jnp.int32, (tk, tk), 1)
    causal = kk <= qq
    kd = 2 * qi
    for i in range(nstrip):
        if i % 2 == 0:
            masked_update(i, sa_ref[:, strip(i)], causal, vt_ref[kd], chunk_offset(kd))
        else:
            qk(kd + 1, i, sb_ref, cmax_b)
            update(i, sa_ref[:, strip(i)], cmax_a[:, strip(i)], vt_ref[kd], chunk_offset(kd))
    for i in range(1, nstrip, 2):
        masked_update(i, sb_ref[:, strip(i)], causal, vt_ref[kd + 1], chunk_offset(kd + 1))

    lam =(jnp.exp(jnp.sum(lq1_ref[...] * lk1_ref[...], axis=-1, keepdims=True))
           - jnp.exp(jnp.sum(lq2_ref[...] * lk2_ref[...], axis=-1, keepdims=True)) + LAM_INIT)
    acc = acc_ref[...]
    o = acc[0:A_DV, :] / acc[A_DV:A_DV + 1, :]
    oa = (o[:, :tq] - lam * o[:, tq:]).T
    ms = jnp.mean(oa * oa, axis=-1, keepdims=True)
    oa = oa * lax.rsqrt(ms + EPS) * sg_ref[...] * (1.0 - LAM_INIT)
    g = g_ref[...]
    o_ref[...] = (oa * (g * jax.nn.sigmoid(g))).astype(o_ref.dtype)


def _slope_table():
    slope2 = (LOG2E * 2.0 ** (-8.0 * np.arange(1, A_HEADS + 1) / A_HEADS)).astype(np.float32)

    def top16(v):
        return (v.astype(np.float32).view(np.uint32) & np.uint32(0xFFFF0000)).view(np.float32)

    b1 = top16(slope2)
    b2 = top16(slope2 - b1)
    b3 = top16(slope2 - b1 - b2)
    return np.stack([slope2, b1, b2, b3], axis=1)


def _attention(slopes, lq1, lk1, lq2, lk2, subln_g, qkv, kv0, rest, tq):
    b, s, _ = qkv.shape
    assert s % tq == 0 and tq % 2 == 0
    tk = tq // 2
    nh = A_HEADS
    small = lambda w: pl.BlockSpec((1, w), lambda bi, hi, qi: (0, 0))
    return pl.pallas_call(
        functools.partial(_attn_kernel, tq=tq, tk=tk),
        grid=(b, nh, s // tq),
        in_specs=[
            pl.BlockSpec(memory_space=pltpu.SMEM),
            small(A_DK), small(A_DK), small(A_DK), small(A_DK), small(A_DV),
            pl.BlockSpec((None, tq, 128), lambda bi, hi, qi: (bi, qi, hi)),
            pl.BlockSpec((None, s, 128), lambda bi, hi, qi: (bi, 0, nh + hi)),
            pl.BlockSpec((None, s, 128), lambda bi, hi, qi: (bi, 0, 2 * nh + hi)),
            pl.BlockSpec((PREFIX, 128), lambda bi, hi, qi: (0, hi)),
            pl.BlockSpec((PREFIX, 128), lambda bi, hi, qi: (0, nh + hi)),
            pl.BlockSpec((None, tq, 128), lambda bi, hi, qi: (bi, qi, OFF_G // 128 + hi)),
        ],
        out_specs=pl.BlockSpec((None, tq, 128), lambda bi, hi, qi: (bi, qi, hi)),
        out_shape=jax.ShapeDtypeStruct((b, s, V_W), BF16),
        scratch_shapes=[pltpu.VMEM((s // tk, A_DV + ONES_ROWS, tk), BF16),
                        pltpu.VMEM((A_DV + ONES_ROWS, PREFIX), BF16),
                        pltpu.VMEM((tk, 128), BF16),
                        pltpu.VMEM((256, 2 * tq), BF16),
                        pltpu.VMEM((tk, 2 * tq), F32),
                        pltpu.VMEM((tk, 2 * tq), F32),
                        pltpu.VMEM((1, 2 * tq), F32),
                        pltpu.VMEM((1, 2 * tq), F32),
                        pltpu.VMEM((1, 2 * tq), F32),
                        pltpu.VMEM((A_DV + ONES_ROWS, 2 * tq), F32)],
        compiler_params=_cparams(("parallel", "parallel", "arbitrary")),
        name="diff_attention",
    )(slopes, lq1, lk1, lq2, lk2, subln_g, qkv, qkv, qkv, kv0, kv0, rest)


def _sigmoid(x):
    return 0.5 * jnp.tanh(0.5 * x) + 0.5


def _silu(x):
    return x * _sigmoid(x)


def _split3(x):
    def top16(v):
        bits = lax.bitcast_convert_type(v, jnp.uint32) & jnp.uint32(0xFFFF0000)
        return lax.bitcast_convert_type(bits, F32)
    p1 = top16(x)
    r1 = x - p1
    p2 = top16(r1)
    return p1.astype(BF16), p2.astype(BF16), (r1 - p2).astype(BF16)


def _ssd_kernel(z_ref, xbc_ref, dt_ref, cw_ref, cb_ref, dtb_ref, alog_ref, dskip_ref, ng_ref,
                exp_ref, st0_ref, tail0_ref,
                y_ref, stout_ref, tailout_ref,
                hist_ref, state_ref, *, q, n_pad):
    c = pl.program_id(1)

    @pl.when(c == 0)
    def _():
        hist_ref[0:8, :] = tail0_ref[...]
        state_ref[...] = st0_ref[...]

    hist_ref[8:8 + q, :] = xbc_ref[...]
    conv = cb_ref[...]
    for j in range(S_CONV):
        conv = conv + cw_ref[j:j + 1, :] * hist_ref[8 - (S_CONV - 1) + j:8 - (S_CONV - 1) + j + q, :]
    hist_ref[0:8, :] = hist_ref[q:q + 8, :]
    xbc = _silu(conv)

    dt_raw = dt_ref[:, :S_HEADS] + dtb_ref[...]
    dt = jnp.maximum(dt_raw, 0.0) + jnp.log(1.0 + jnp.exp(-jnp.abs(dt_raw)))
    if n_pad:
        row = lax.broadcasted_iota(jnp.int32, (q, 1), 0)
        valid = row >= n_pad
        xbc = jnp.where(valid, xbc, 0.0)
        dt = jnp.where(valid, dt, 0.0)
    a = -jnp.exp(alog_ref[...])
    da = dt * a

    r_i = lax.broadcasted_iota(jnp.int32, (q, q), 0)
    c_i = lax.broadcasted_iota(jnp.int32, (q, q), 1)
    tril = c_i <= r_i
    acs = jnp.dot(tril.astype(F32), da, preferred_element_type=F32, precision=lax.Precision.HIGHEST)
    acs_t = acs.T

    pieces = _split3(jnp.concatenate([dt, acs], axis=0))
    wide = jnp.dot(jnp.concatenate(pieces, axis=1), exp_ref[...], preferred_element_type=F32)
    dt_w = wide[:q]
    acs_w = wide[q:]
    acs_last = acs_w[q - 1:q, :]

    xs = xbc[:, :D_SSM]
    bm = xbc[:, D_SSM:D_SSM + BC_W].astype(BF16)
    cm = xbc[:, D_SSM + BC_W:].astype(BF16)
    xdt = xs * dt_w
    xdec_b = (xdt * jnp.exp(acs_last - acs_w)).astype(BF16)
    state_old = state_ref[...]
    state_b = state_old.astype(BF16)
    lane = lax.broadcasted_iota(jnp.int32, (q, 128), 1)

    y_groups = []
    st_parts = []
    hpg = S_HEADS // S_GROUPS
    for g in range(S_GROUPS):
        bg = bm[:, g * S_STATE:(g + 1) * S_STATE]
        cg = cm[:, g * S_STATE:(g + 1) * S_STATE]
        cbm = lax.dot_general(cg, bg, (((1,), (1,)), ((), ())), preferred_element_type=F32)
        gs = slice(g * GROUP_W, (g + 1) * GROUP_W)
        y_off = jnp.dot(cg, state_b[:, gs], preferred_element_type=F32)
        st_parts.append(lax.dot_general(bg, xdec_b[:, gs], (((0,), (0,)), ((), ())),
                                        preferred_element_type=F32))
        pairs = []
        for pair in range(hpg // 2):
            h0 = g * hpg + 2 * pair
            cols = slice(h0 * S_HEADDIM, (h0 + 2) * S_HEADDIM)
            xp = xdt[:, cols].astype(BF16)
            ys = []
            for hh in (h0, h0 + 1):
                seg = acs[:, hh:hh + 1] - acs_t[hh:hh + 1, :]
                lmat = jnp.exp(jnp.where(tril, seg, -jnp.inf))
                ys.append(jnp.dot((cbm * lmat).astype(BF16), xp, preferred_element_type=F32))
            pairs.append(jnp.where(lane < S_HEADDIM, ys[0], ys[1]))
        y_groups.append(jnp.concatenate(pairs, axis=1) + y_off * jnp.exp(acs_w[:, gs]))

    y = jnp.concatenate(y_groups, axis=1)
    state_new = jnp.exp(acs_last) * state_old + jnp.concatenate(st_parts, axis=1)
    state_ref[...] = state_new

    y = y + xs * dskip_ref[...]
    z = z_ref[...]
    yg = y * _silu(z)
    outs = []
    for g in range(S_GROUPS):
        blk = yg[:, g * GROUP_W:(g + 1) * GROUP_W]
        ms = jnp.mean(blk * blk, axis=-1, keepdims=True)
        outs.append(blk * lax.rsqrt(ms + EPS))
    y_ref[...] = (jnp.concatenate(outs, axis=1) * ng_ref[...]).astype(y_ref.dtype)

    @pl.when(c == pl.num_programs(1) - 1)
    def _():
        stout_ref[...] = state_new
        tailout_ref[...] = hist_ref[0:8, :]


def _ssd(rest, conv_w, conv_b, dt_bias, a_log, dskip_w, norm_g, expand, state0, tail0, q, n_pad):
    b, s, _ = rest.shape
    q = min(q, s)
    const = lambda shp: pl.BlockSpec(shp, lambda bi, ci: (0,) * len(shp))
    return pl.pallas_call(
        functools.partial(_ssd_kernel, q=q, n_pad=n_pad),
        grid=(b, s // q),
        in_specs=[
            pl.BlockSpec((None, q, D_SSM), lambda bi, ci: (bi, ci, OFF_Z // D_SSM)),
            pl.BlockSpec((None, q, CONV_DIM), lambda bi, ci: (bi, ci, OFF_XBC // CONV_DIM)),
            pl.BlockSpec((None, q, 128), lambda bi, ci: (bi, ci, OFF_DT // 128)),
            const((S_CONV, CONV_DIM)), const((1, CONV_DIM)), const((1, S_HEADS)), const((1, S_HEADS)),
            const((1, D_SSM)), const((1, D_SSM)), const((3 * S_HEADS, D_SSM)),
            const((S_STATE, D_SSM)), const((8, CONV_DIM)),
        ],
        out_specs=[
            pl.BlockSpec((None, q, D_SSM), lambda bi, ci: (bi, ci, 0)),
            pl.BlockSpec((None, S_STATE, D_SSM), lambda bi, ci: (bi, 0, 0)),
            pl.BlockSpec((None, 8, CONV_DIM), lambda bi, ci: (bi, 0, 0)),
        ],
        out_shape=[
            jax.ShapeDtypeStruct((b, s, D_SSM), BF16),
            jax.ShapeDtypeStruct((b, S_STATE, D_SSM), F32),
            jax.ShapeDtypeStruct((b, 8, CONV_DIM), F32),
        ],
        scratch_shapes=[pltpu.VMEM((8 + q, CONV_DIM), F32), pltpu.VMEM((S_STATE, D_SSM), F32)],
        compiler_params=_cparams(("parallel", "arbitrary")),
        name="ssd",
    )(rest, rest, rest, conv_w, conv_b, dt_bias, a_log, dskip_w, norm_g, expand, state0, tail0)


def _out_kernel(x_ref, oa_ref, yg_ref, ma_ref, ms_ref, wa_ref, ws_ref, wo_ref, fg_ref, o_ref):
    y_att = jnp.dot(oa_ref[...], wa_ref[...], preferred_element_type=F32)
    y_ssm = jnp.dot(yg_ref[...], ws_ref[...], preferred_element_type=F32)
    merged = jax.nn.sigmoid(ma_ref[...]) * y_att + jax.nn.sigmoid(ms_ref[...]) * y_ssm
    hres = x_ref[...] + jnp.dot(merged.astype(BF16), wo_ref[...], preferred_element_type=F32)
    ms = jnp.mean(hres * hres, axis=-1, keepdims=True)
    o_ref[...] = hres * lax.rsqrt(ms + EPS) * fg_ref[...]


def _out(x, oa, yg, rest, w_a, w_s, w_o, final_g, tm):
    m, d = x.shape
    tm = min(tm, m)
    once = pl.Buffered(1)
    return pl.pallas_call(
        _out_kernel,
        grid=(m // tm,),
        in_specs=[
            pl.BlockSpec((tm, d), lambda i: (i, 0)),
            pl.BlockSpec((tm, V_W), lambda i: (i, 0)),
            pl.BlockSpec((tm, D_SSM), lambda i: (i, 0)),
            pl.BlockSpec((tm, d), lambda i: (i, OFF_MATT // D_MODEL)),
            pl.BlockSpec((tm, d), lambda i: (i, OFF_MSSM // D_MODEL)),
            pl.BlockSpec((V_W, d), lambda i: (0, 0), pipeline_mode=once),
            pl.BlockSpec((D_SSM, d), lambda i: (0, 0), pipeline_mode=once),
            pl.BlockSpec((d, d), lambda i: (0, 0), pipeline_mode=once),
            pl.BlockSpec((1, d), lambda i: (0, 0)),
        ],
        out_specs=pl.BlockSpec((tm, d), lambda i: (i, 0)),
        out_shape=jax.ShapeDtypeStruct((m, d), F32),
        compiler_params=_cparams(("parallel",)),
        name="out_proj",
    )(x, oa, yg, rest, rest, w_a, w_s, w_o, final_g)


def kernel(x, meta, norm_g, w_in, conv_w, conv_b, dt_bias, a_log, d_skip, ssm_norm_g,
           lam_q1, lam_k1, lam_q2, lam_k2, subln_g, w_br_attn, w_br_ssm, w_out, final_g):
    b, s, d = x.shape
    assert d == D_MODEL and norm_g.shape[0] == 1 and s % 128 == 0

    w = jnp.swapaxes(w_in[0], 0, 1)
    o_q, o_k, o_v, o_g = 0, Q_W, 2 * Q_W, 2 * Q_W + V_W
    o_z = o_g + V_W
    o_xbc = o_z + D_SSM
    o_dt = o_xbc + CONV_DIM
    o_ma = o_dt + S_HEADS
    o_ms = o_ma + D_MODEL
    w_qkv = jnp.concatenate([w[o_q:o_k] * (A_DK ** -0.5 * LOG2E), w[o_k:o_g]], axis=0).astype(BF16)
    w_rest = jnp.concatenate([
        w[o_z:o_xbc], w[o_ma:o_ms], w[o_ms:o_ms + D_MODEL], w[o_xbc:o_dt], w[o_g:o_z],
        w[o_dt:o_ma], jnp.zeros((DT_W - S_HEADS, d), w.dtype)], axis=0).astype(BF16)
    w_kv = w_qkv[Q_W:]

    g_in = norm_g[0][None, :]
    xf = x.reshape(b * s, d)
    h0 = jnp.concatenate([jnp.zeros((N_PAD, d), x.dtype), meta.astype(x.dtype)], axis=0)

    qkv = _inproj(xf, g_in, w_qkv, BF16, 1024, 1536).reshape(b, s, 3 * Q_W)
    rest = _inproj(xf, g_in, w_rest, F32, 1024, 1536).reshape(b, s, REST_W)
    kv0 = _inproj(h0, g_in, w_kv, BF16, PREFIX, 1024)
    rest0 = _inproj(h0, g_in, w_rest, F32, PREFIX, 1536).reshape(1, PREFIX, REST_W)

    slopes = jnp.asarray(_slope_table())
    row = lambda v: v.astype(F32).reshape(1, -1)
    oa = _attention(slopes, row(lam_q1[0]), row(lam_k1[0]), row(lam_q2[0]), row(lam_k2[0]), row(subln_g[0]),
                    qkv, kv0, rest, min(1024, s))

    dskip_w = jnp.repeat(d_skip[0].astype(F32), S_HEADDIM)[None, :]
    expand = jnp.tile(jnp.repeat(jnp.eye(S_HEADS, dtype=BF16), S_HEADDIM, axis=1), (3, 1))
    ssd_args = (conv_w[0].astype(F32), row(conv_b[0]), row(dt_bias[0]), row(a_log[0]), dskip_w,
                row(ssm_norm_g[0]), expand)
    zero_state = jnp.zeros((S_STATE, D_SSM), F32)
    zero_tail = jnp.zeros((8, CONV_DIM), F32)
    _, state0, tail0 = _ssd(rest0, *ssd_args, zero_state, zero_tail, PREFIX, N_PAD)
    yg, _, _ = _ssd(rest, *ssd_args, state0[0], tail0[0], 128, 0)

    out = _out(xf, oa.reshape(b * s, V_W), yg.reshape(b * s, D_SSM), rest.reshape(b * s, REST_W),
               w_br_attn[0].astype(BF16), w_br_ssm[0].astype(BF16), w_out[0].astype(BF16),
               row(final_g), 256)
    return out.reshape(b, s, d)
```

```python
import functools
import math

import jax
import jax.numpy as jnp
import numpy as np
from jax import lax
from jax.experimental import pallas as pl
from jax.experimental.pallas import tpu as pltpu

F32 = jnp.float32
BF16 = jnp.bfloat16

D_MODEL = 2048
N_META = 16
PREFIX = 128
N_PAD = PREFIX - N_META
A_HEADS = 8
A_DK = 64
A_DV = 128
Q_W = A_HEADS * 2 * A_DK
V_W = A_HEADS * A_DV
D_SSM = 2048
S_HEADDIM = 64
S_HEADS = D_SSM // S_HEADDIM
S_GROUPS = 4
S_STATE = 128
S_CONV = 4
GROUP_W = D_SSM // S_GROUPS
BC_W = S_GROUPS * S_STATE
CONV_DIM = D_SSM + 2 * BC_W
DT_W = 512
EPS = 1e-6
NEG = -1e30
LAM_INIT = 0.8 - 0.6 * math.exp(-0.3 * 0)

OFF_Z = 0
OFF_MATT = OFF_Z + D_SSM
OFF_MSSM = OFF_MATT + D_MODEL
OFF_XBC = OFF_MSSM + D_MODEL
OFF_G = OFF_XBC + CONV_DIM
OFF_DT = OFF_G + V_W
REST_W = OFF_DT + DT_W

VMEM_LIMIT = 56 * 1024 * 1024


def _cparams(sem):
    return pltpu.CompilerParams(dimension_semantics=sem, vmem_limit_bytes=VMEM_LIMIT)


def _inproj_kernel(x_ref, g_ref, w_ref, o_ref, u_ref):
    @pl.when(pl.program_id(1) == 0)
    def _():
        x = x_ref[...]
        ms = jnp.mean(x * x, axis=-1, keepdims=True)
        u_ref[...] = (x * lax.rsqrt(ms + EPS) * g_ref[...]).astype(BF16)

    o_ref[...] = lax.dot_general(u_ref[...], w_ref[...], _NT, preferred_element_type=F32).astype(o_ref.dtype)


def _inproj(x, g, w, out_dtype, tm, tn):
    m, d = x.shape
    n = w.shape[0]
    tm = min(tm, m)
    return pl.pallas_call(
        _inproj_kernel,
        grid=(m // tm, n // tn),
        in_specs=[
            pl.BlockSpec((tm, d), lambda i, j: (i, 0)),
            pl.BlockSpec((1, d), lambda i, j: (0, 0)),
            pl.BlockSpec((tn, d), lambda i, j: (j, 0)),
        ],
        out_specs=pl.BlockSpec((tm, tn), lambda i, j: (i, j)),
        out_shape=jax.ShapeDtypeStruct((m, n), out_dtype),
        scratch_shapes=[pltpu.VMEM((tm, d), BF16)],
        compiler_params=_cparams(("parallel", "arbitrary")),
        name="in_proj",
    )(x, g, w)


ONES_ROWS = 16
LOG2E = math.log2(math.e)
_NT = (((1,), (1,)), ((), ()))


def _attn_kernel(slope_ref, lq1_ref, lk1_ref, lq2_ref, lk2_ref, sg_ref,
                 q_ref, k_ref, v_ref, km_ref, vm_ref, g_ref, o_ref,
                 vt_ref, vmt_ref, kpos_ref, qz_ref, sa_ref, sb_ref, cmax_a, cmax_b, m_ref, acc_ref,
                 *, tq, tk):
    h = pl.program_id(1)
    qi = pl.program_id(2)
    slope = slope_ref[h, 0]
    nchunk = k_ref.shape[0] // tk
    r_e = lax.broadcasted_iota(jnp.int32, (128, 128), 0)
    c_e = lax.broadcasted_iota(jnp.int32, (128, 128), 1)
    eye = (r_e == c_e).astype(BF16)

    @pl.when(qi == 0)
    def _():
        ones = jnp.ones((ONES_ROWS, tk), BF16)

        def tr(j, carry):
            off = pl.multiple_of(j * tk, tk)
            vt = lax.dot_general(eye, v_ref[pl.ds(off, tk), :], _NT, preferred_element_type=F32)
            vt_ref[j, 0:A_DV, :] = vt.astype(BF16)
            vt_ref[j, A_DV:, :] = ones
            return carry
        lax.fori_loop(0, nchunk, tr, 0, unroll=4)
        vmt = lax.dot_general(eye, vm_ref[...], _NT, preferred_element_type=F32)
        vmt_ref[0:A_DV, :] = vmt.astype(BF16)
        vmt_ref[A_DV:, :] = ones[:, :PREFIX]

        row = lax.broadcasted_iota(jnp.int32, (tk, 128), 0)
        lane = lax.broadcasted_iota(jnp.int32, (tk, 128), 1)
        lo = row % 256
        kpos = jnp.where(lane < 3, lo, jnp.where(lane < 6, row - lo, 0))
        kpos_ref[...] = kpos.astype(F32).astype(BF16)
        arow = lax.broadcasted_iota(jnp.int32, (128, 2 * tq), 0)
        piece = jnp.where(arow % 3 == 0, slope_ref[h, 1], jnp.where(arow % 3 == 1, slope_ref[h, 2], slope_ref[h, 3]))
        qz_ref[128:256, :] = jnp.where(arow < 6, piece, 0.0).astype(BF16)

    qt = lax.dot_general(eye, q_ref[...], _NT, preferred_element_type=F32).astype(BF16)
    drow = lax.broadcasted_iota(jnp.int32, qt.shape, 0)
    zero = jnp.zeros_like(qt)
    qz_ref[0:128, :tq] = jnp.where(drow < A_DK, qt, zero)
    qz_ref[0:128, tq:] = jnp.where(drow >= A_DK, qt, zero)

    m_ref[...] = jnp.full(m_ref.shape, NEG, F32)
    acc_ref[...] = jnp.zeros(acc_ref.shape, F32)

    nstrip = 2 * tq // tk
    strip = lambda i: slice(i * tk, (i + 1) * tk)

    def scores(k, nk, i):
        lhs = jnp.concatenate([k, kpos_ref[0:nk, :]], axis=1)
        return jnp.dot(lhs, qz_ref[:, strip(i)], preferred_element_type=F32)

    def qk(kc, i, s_buf, cmax_ref):
        off = pl.multiple_of(kc * tk, tk)
        s = scores(k_ref[pl.ds(off, tk), :], tk, i)
        s_buf[:, strip(i)] = s
        cmax_ref[:, strip(i)] = jnp.max(s, axis=0, keepdims=True)

    def update(i, s, cmax, vt, c0):
        m_old = m_ref[:, strip(i)]
        m_new = jnp.maximum(m_old, cmax + c0)
        p = jnp.exp2(s - (m_new - c0)).astype(BF16)
        alpha = jnp.exp2(m_old - m_new)
        acc_ref[:, strip(i)] = alpha * acc_ref[:, strip(i)] + jnp.dot(vt, p, preferred_element_type=F32)
        m_ref[:, strip(i)] = m_new

    def masked_update(i, s, keep, vt, c0):
        s = jnp.where(keep, s, NEG)
        update(i, s, jnp.max(s, axis=0, keepdims=True), vt, c0)

    def chunk_offset(kc):
        return -slope * (qi * tq - kc * tk).astype(F32)

    s_prefix = [scores(km_ref[...], PREFIX, i) for i in range(nstrip)]
    for i in range(nstrip):
        qk(0, i, sa_ref, cmax_a)
    krow = lax.broadcasted_iota(jnp.int32, (PREFIX, tk), 0)
    for i in range(nstrip):
        masked_update(i, s_prefix[i], krow >= N_PAD, vmt_ref[...], -slope * ((qi * tq).astype(F32) + PREFIX))

    def pair(j):
        kc = 2 * j
        for i in range(nstrip):
            qk(kc + 1, i, sb_ref, cmax_b)
            update(i, sa_ref[:, strip(i)], cmax_a[:, strip(i)], vt_ref[kc], chunk_offset(kc))
        for i in range(nstrip):
            qk(kc + 2, i, sa_ref, cmax_a)
            update(i, sb_ref[:, strip(i)], cmax_b[:, strip(i)], vt_ref[kc + 1], chunk_offset(kc + 1))

    def two_pairs(t, carry):
        pair(2 * t)
        pair(2 * t + 1)
        return carry

    lax.fori_loop(0, qi // 2, two_pairs, 0)

    @pl.when(qi % 2 == 1)
    def _():
        pair(qi - 1)

    kk = lax.broadcasted_iota(jnp.int32, (tk, tk), 0)
    qq = lax.broadcasted_iota(jnp.int32, (tk, tk), 1)
    causal = kk <= qq
    kd = 2 * qi
    for i in range(nstrip):
        if i % 2 == 0:
            masked_update(i, sa_ref[:, strip(i)], causal, vt_ref[kd], chunk_offset(kd))
        else:
            qk(kd + 1, i, sb_ref, cmax_b)
            update(i, sa_ref[:, strip(i)], cmax_a[:, strip(i)], vt_ref[kd], chunk_offset(kd))
    for i in range(1, nstrip, 2):
        masked_update(i, sb_ref[:, strip(i)], causal, vt_ref[kd + 1], chunk_offset(kd + 1))

    lam =(jnp.exp(jnp.sum(lq1_ref[...] * lk1_ref[...], axis=-1, keepdims=True))
           - jnp.exp(jnp.sum(lq2_ref[...] * lk2_ref[...], axis=-1, keepdims=True)) + LAM_INIT)
    acc = acc_ref[...]
    o = acc[0:A_DV, :] / acc[A_DV:A_DV + 1, :]
    oa = (o[:, :tq] - lam * o[:, tq:]).T
    ms = jnp.mean(oa * oa, axis=-1, keepdims=True)
    oa = oa * lax.rsqrt(ms + EPS) * sg_ref[...] * (1.0 - LAM_INIT)
    g = g_ref[...]
    o_ref[...] = (oa * (g * jax.nn.sigmoid(g))).astype(o_ref.dtype)


def _slope_table():
    slope2 = (LOG2E * 2.0 ** (-8.0 * np.arange(1, A_HEADS + 1) / A_HEADS)).astype(np.float32)

    def top16(v):
        return (v.astype(np.float32).view(np.uint32) & np.uint32(0xFFFF0000)).view(np.float32)

    b1 = top16(slope2)
    b2 = top16(slope2 - b1)
    b3 = top16(slope2 - b1 - b2)
    return np.stack([slope2, b1, b2, b3], axis=1)


def _attention(slopes, lq1, lk1, lq2, lk2, subln_g, qkv, kv0, rest, tq):
    b, s, _ = qkv.shape
    assert s % tq == 0 and tq % 2 == 0
    tk = tq // 2
    nh = A_HEADS
    small = lambda w: pl.BlockSpec((1, w), lambda bi, hi, qi: (0, 0))
    return pl.pallas_call(
        functools.partial(_attn_kernel, tq=tq, tk=tk),
        grid=(b, nh, s // tq),
        in_specs=[
            pl.BlockSpec(memory_space=pltpu.SMEM),
            small(A_DK), small(A_DK), small(A_DK), small(A_DK), small(A_DV),
            pl.BlockSpec((None, tq, 128), lambda bi, hi, qi: (bi, qi, hi)),
            pl.BlockSpec((None, s, 128), lambda bi, hi, qi: (bi, 0, nh + hi)),
            pl.BlockSpec((None, s, 128), lambda bi, hi, qi: (bi, 0, 2 * nh + hi)),
            pl.BlockSpec((PREFIX, 128), lambda bi, hi, qi: (0, hi)),
            pl.BlockSpec((PREFIX, 128), lambda bi, hi, qi: (0, nh + hi)),
            pl.BlockSpec((None, tq, 128), lambda bi, hi, qi: (bi, qi, OFF_G // 128 + hi)),
        ],
        out_specs=pl.BlockSpec((None, tq, 128), lambda bi, hi, qi: (bi, qi, hi)),
        out_shape=jax.ShapeDtypeStruct((b, s, V_W), BF16),
        scratch_shapes=[pltpu.VMEM((s // tk, A_DV + ONES_ROWS, tk), BF16),
                        pltpu.VMEM((A_DV + ONES_ROWS, PREFIX), BF16),
                        pltpu.VMEM((tk, 128), BF16),
                        pltpu.VMEM((256, 2 * tq), BF16),
                        pltpu.VMEM((tk, 2 * tq), F32),
                        pltpu.VMEM((tk, 2 * tq), F32),
                        pltpu.VMEM((1, 2 * tq), F32),
                        pltpu.VMEM((1, 2 * tq), F32),
                        pltpu.VMEM((1, 2 * tq), F32),
                        pltpu.VMEM((A_DV + ONES_ROWS, 2 * tq), F32)],
        compiler_params=_cparams(("parallel", "parallel", "arbitrary")),
        name="diff_attention",
    )(slopes, lq1, lk1, lq2, lk2, subln_g, qkv, qkv, qkv, kv0, kv0, rest)


def _sigmoid(x):
    return 0.5 * jnp.tanh(0.5 * x) + 0.5


def _silu(x):
    return x * _sigmoid(x)


def _split3(x):
    def top16(v):
        bits = lax.bitcast_convert_type(v, jnp.uint32) & jnp.uint32(0xFFFF0000)
        return lax.bitcast_convert_type(bits, F32)
    p1 = top16(x)
    r1 = x - p1
    p2 = top16(r1)
    return p1.astype(BF16), p2.astype(BF16), (r1 - p2).astype(BF16)


def _ssd_kernel(z_ref, xbc_ref, dt_ref, cw_ref, cb_ref, dtb_ref, alog_ref, dskip_ref, ng_ref,
                exp_ref, st0_ref, tail0_ref,
                y_ref, stout_ref, tailout_ref,
                hist_ref, state_ref, *, q, n_pad):
    c = pl.program_id(1)

    @pl.when(c == 0)
    def _():
        hist_ref[0:8, :] = tail0_ref[...]
        state_ref[...] = st0_ref[...]

    hist_ref[8:8 + q, :] = xbc_ref[...]
    xpad = hist_ref[...]
    conv = cb_ref[...] + cw_ref[S_CONV - 1:S_CONV, :] * xpad[8:, :]
    for back in range(1, S_CONV):
        conv = conv + cw_ref[S_CONV - 1 - back:S_CONV - back, :] * pltpu.roll(xpad, back, 0)[8:, :]
    hist_ref[0:8, :] = hist_ref[q:q + 8, :]
    xbc = _silu(conv)

    dt_raw = dt_ref[:, :S_HEADS] + dtb_ref[...]
    dt = jnp.maximum(dt_raw, 0.0) + jnp.log(1.0 + jnp.exp(-jnp.abs(dt_raw)))
    if n_pad:
        row = lax.broadcasted_iota(jnp.int32, (q, 1), 0)
        valid = row >= n_pad
        xbc = jnp.where(valid, xbc, 0.0)
        dt = jnp.where(valid, dt, 0.0)
    a = -jnp.exp(alog_ref[...])
    da = dt * a

    r_i = lax.broadcasted_iota(jnp.int32, (q, q), 0)
    c_i = lax.broadcasted_iota(jnp.int32, (q, q), 1)
    tril = c_i <= r_i
    acs = jnp.dot(tril.astype(F32), da, preferred_element_type=F32, precision=lax.Precision.HIGHEST)
    acs_t = acs.T

    pieces = _split3(jnp.concatenate([dt, acs], axis=0))
    wide = jnp.dot(jnp.concatenate(pieces, axis=1), exp_ref[...], preferred_element_type=F32)
    dt_w = wide[:q]
    acs_w = wide[q:]
    acs_last = acs_w[q - 1:q, :]

    xs = xbc[:, :D_SSM]
    bm = xbc[:, D_SSM:D_SSM + BC_W].astype(BF16)
    cm = xbc[:, D_SSM + BC_W:].astype(BF16)
    xdt = xs * dt_w
    xdec_b = (xdt * jnp.exp(acs_last - acs_w)).astype(BF16)
    state_old = state_ref[...]
    state_b = state_old.astype(BF16)
    lane = lax.broadcasted_iota(jnp.int32, (q, 128), 1)

    y_groups = []
    st_parts = []
    hpg = S_HEADS // S_GROUPS
    for g in range(S_GROUPS):
        bg = bm[:, g * S_STATE:(g + 1) * S_STATE]
        cg = cm[:, g * S_STATE:(g + 1) * S_STATE]
        cbm = lax.dot_general(cg, bg, (((1,), (1,)), ((), ())), preferred_element_type=F32)
        gs = slice(g * GROUP_W, (g + 1) * GROUP_W)
        y_off = jnp.dot(cg, state_b[:, gs], preferred_element_type=F32)
        st_parts.append(lax.dot_general(bg, xdec_b[:, gs], (((0,), (0,)), ((), ())),
                                        preferred_element_type=F32))
        pairs = []
        for pair in range(hpg // 2):
            h0 = g * hpg + 2 * pair
            cols = slice(h0 * S_HEADDIM, (h0 + 2) * S_HEADDIM)
            xp = xdt[:, cols].astype(BF16)
            ys = []
            for hh in (h0, h0 + 1):
                seg = acs[:, hh:hh + 1] - acs_t[hh:hh + 1, :]
                lmat = jnp.exp(jnp.where(tril, seg, -jnp.inf))
                ys.append(jnp.dot((cbm * lmat).astype(BF16), xp, preferred_element_type=F32))
            pairs.append(jnp.where(lane < S_HEADDIM, ys[0], ys[1]))
        y_groups.append(jnp.concatenate(pairs, axis=1) + y_off * jnp.exp(acs_w[:, gs]))

    y = jnp.concatenate(y_groups, axis=1)
    state_new = jnp.exp(acs_last) * state_old + jnp.concatenate(st_parts, axis=1)
    state_ref[...] = state_new

    y = y + xs * dskip_ref[...]
    z = z_ref[...]
    yg = y * _silu(z)
    outs = []
    for g in range(S_GROUPS):
        blk = yg[:, g * GROUP_W:(g + 1) * GROUP_W]
        ms = jnp.mean(blk * blk, axis=-1, keepdims=True)
        outs.append(blk * lax.rsqrt(ms + EPS))
    y_ref[...] = (jnp.concatenate(outs, axis=1) * ng_ref[...]).astype(y_ref.dtype)

    @pl.when(c == pl.num_programs(1) - 1)
    def _():
        stout_ref[...] = state_new
        tailout_ref[...] = hist_ref[0:8, :]


def _ssd(rest, conv_w, conv_b, dt_bias, a_log, dskip_w, norm_g, expand, state0, tail0, q, n_pad):
    b, s, _ = rest.shape
    q = min(q, s)
    const = lambda shp: pl.BlockSpec(shp, lambda bi, ci: (0,) * len(shp))
    return pl.pallas_call(
        functools.partial(_ssd_kernel, q=q, n_pad=n_pad),
        grid=(b, s // q),
        in_specs=[
            pl.BlockSpec((None, q, D_SSM), lambda bi, ci: (bi, ci, OFF_Z // D_SSM)),
            pl.BlockSpec((None, q, CONV_DIM), lambda bi, ci: (bi, ci, OFF_XBC // CONV_DIM)),
            pl.BlockSpec((None, q, 128), lambda bi, ci: (bi, ci, OFF_DT // 128)),
            const((S_CONV, CONV_DIM)), const((1, CONV_DIM)), const((1, S_HEADS)), const((1, S_HEADS)),
            const((1, D_SSM)), const((1, D_SSM)), const((3 * S_HEADS, D_SSM)),
            const((S_STATE, D_SSM)), const((8, CONV_DIM)),
        ],
        out_specs=[
            pl.BlockSpec((None, q, D_SSM), lambda bi, ci: (bi, ci, 0)),
            pl.BlockSpec((None, S_STATE, D_SSM), lambda bi, ci: (bi, 0, 0)),
            pl.BlockSpec((None, 8, CONV_DIM), lambda bi, ci: (bi, 0, 0)),
        ],
        out_shape=[
            jax.ShapeDtypeStruct((b, s, D_SSM), BF16),
            jax.ShapeDtypeStruct((b, S_STATE, D_SSM), F32),
            jax.ShapeDtypeStruct((b, 8, CONV_DIM), F32),
        ],
        scratch_shapes=[pltpu.VMEM((8 + q, CONV_DIM), F32), pltpu.VMEM((S_STATE, D_SSM), F32)],
        compiler_params=_cparams(("parallel", "arbitrary")),
        name="ssd",
    )(rest, rest, rest, conv_w, conv_b, dt_bias, a_log, dskip_w, norm_g, expand, state0, tail0)


def _out_kernel(x_ref, oa_ref, yg_ref, ma_ref, ms_ref, wa_ref, ws_ref, wo_ref, fg_ref, o_ref):
    y_att = jnp.dot(oa_ref[...], wa_ref[...], preferred_element_type=F32)
    y_ssm = jnp.dot(yg_ref[...], ws_ref[...], preferred_element_type=F32)
    merged = jax.nn.sigmoid(ma_ref[...]) * y_att + jax.nn.sigmoid(ms_ref[...]) * y_ssm
    hres = x_ref[...] + jnp.dot(merged.astype(BF16), wo_ref[...], preferred_element_type=F32)
    ms = jnp.mean(hres * hres, axis=-1, keepdims=True)
    o_ref[...] = hres * lax.rsqrt(ms + EPS) * fg_ref[...]


def _out(x, oa, yg, rest, w_a, w_s, w_o, final_g, tm):
    m, d = x.shape
    tm = min(tm, m)
    once = pl.Buffered(1)
    return pl.pallas_call(
        _out_kernel,
        grid=(m // tm,),
        in_specs=[
            pl.BlockSpec((tm, d), lambda i: (i, 0)),
            pl.BlockSpec((tm, V_W), lambda i: (i, 0)),
            pl.BlockSpec((tm, D_SSM), lambda i: (i, 0)),
            pl.BlockSpec((tm, d), lambda i: (i, OFF_MATT // D_MODEL)),
            pl.BlockSpec((tm, d), lambda i: (i, OFF_MSSM // D_MODEL)),
            pl.BlockSpec((V_W, d), lambda i: (0, 0), pipeline_mode=once),
            pl.BlockSpec((D_SSM, d), lambda i: (0, 0), pipeline_mode=once),
            pl.BlockSpec((d, d), lambda i: (0, 0), pipeline_mode=once),
            pl.BlockSpec((1, d), lambda i: (0, 0)),
        ],
        out_specs=pl.BlockSpec((tm, d), lambda i: (i, 0)),
        out_shape=jax.ShapeDtypeStruct((m, d), F32),
        compiler_params=_cparams(("parallel",)),
        name="out_proj",
    )(x, oa, yg, rest, rest, w_a, w_s, w_o, final_g)


def kernel(x, meta, norm_g, w_in, conv_w, conv_b, dt_bias, a_log, d_skip, ssm_norm_g,
           lam_q1, lam_k1, lam_q2, lam_k2, subln_g, w_br_attn, w_br_ssm, w_out, final_g):
    b, s, d = x.shape
    assert d == D_MODEL and norm_g.shape[0] == 1 and s % 128 == 0

    w = jnp.swapaxes(w_in[0], 0, 1)
    o_q, o_k, o_v, o_g = 0, Q_W, 2 * Q_W, 2 * Q_W + V_W
    o_z = o_g + V_W
    o_xbc = o_z + D_SSM
    o_dt = o_xbc + CONV_DIM
    o_ma = o_dt + S_HEADS
    o_ms = o_ma + D_MODEL
    w_qkv = jnp.concatenate([w[o_q:o_k] * (A_DK ** -0.5 * LOG2E), w[o_k:o_g]], axis=0).astype(BF16)
    w_rest = jnp.concatenate([
        w[o_z:o_xbc], w[o_ma:o_ms], w[o_ms:o_ms + D_MODEL], w[o_xbc:o_dt], w[o_g:o_z],
        w[o_dt:o_ma], jnp.zeros((DT_W - S_HEADS, d), w.dtype)], axis=0).astype(BF16)
    w_kv = w_qkv[Q_W:]

    g_in = norm_g[0][None, :]
    xf = x.reshape(b * s, d)
    h0 = jnp.concatenate([jnp.zeros((N_PAD, d), x.dtype), meta.astype(x.dtype)], axis=0)

    qkv = _inproj(xf, g_in, w_qkv, BF16, 1024, 1536).reshape(b, s, 3 * Q_W)
    rest = _inproj(xf, g_in, w_rest, F32, 1024, 1536).reshape(b, s, REST_W)
    kv0 = _inproj(h0, g_in, w_kv, BF16, PREFIX, 1024)
    rest0 = _inproj(h0, g_in, w_rest, F32, PREFIX, 1536).reshape(1, PREFIX, REST_W)

    slopes = jnp.asarray(_slope_table())
    row = lambda v: v.astype(F32).reshape(1, -1)
    oa = _attention(slopes, row(lam_q1[0]), row(lam_k1[0]), row(lam_q2[0]), row(lam_k2[0]), row(subln_g[0]),
                    qkv, kv0, rest, min(1024, s))

    dskip_w = jnp.repeat(d_skip[0].astype(F32), S_HEADDIM)[None, :]
    expand = jnp.tile(jnp.repeat(jnp.eye(S_HEADS, dtype=BF16), S_HEADDIM, axis=1), (3, 1))
    ssd_args = (conv_w[0].astype(F32), row(conv_b[0]), row(dt_bias[0]), row(a_log[0]), dskip_w,
                row(ssm_norm_g[0]), expand)
    zero_state = jnp.zeros((S_STATE, D_SSM), F32)
    zero_tail = jnp.zeros((8, CONV_DIM), F32)
    _, state0, tail0 = _ssd(rest0, *ssd_args, zero_state, zero_tail, PREFIX, N_PAD)
    yg, _, _ = _ssd(rest, *ssd_args, state0[0], tail0[0], 128, 0)

    out = _out(xf, oa.reshape(b * s, V_W), yg.reshape(b * s, D_SSM), rest.reshape(b * s, REST_W),
               w_br_attn[0].astype(BF16), w_br_ssm[0].astype(BF16), w_out[0].astype(BF16),
               row(final_g), 256)
    return out.reshape(b, s, d)
```

```python
import functools
import math

import jax
import jax.numpy as jnp
import numpy as np
from jax import lax
from jax.experimental import pallas as pl
from jax.experimental.pallas import tpu as pltpu

F32 = jnp.float32
BF16 = jnp.bfloat16

D_MODEL = 2048
N_META = 16
PREFIX = 128
N_PAD = PREFIX - N_META
A_HEADS = 8
A_DK = 64
A_DV = 128
Q_W = A_HEADS * 2 * A_DK
V_W = A_HEADS * A_DV
D_SSM = 2048
S_HEADDIM = 64
S_HEADS = D_SSM // S_HEADDIM
S_GROUPS = 4
S_STATE = 128
S_CONV = 4
GROUP_W = D_SSM // S_GROUPS
BC_W = S_GROUPS * S_STATE
CONV_DIM = D_SSM + 2 * BC_W
DT_W = 512
EPS = 1e-6
NEG = -1e30
LAM_INIT = 0.8 - 0.6 * math.exp(-0.3 * 0)

OFF_Z = 0
OFF_MATT = OFF_Z + D_SSM
OFF_MSSM = OFF_MATT + D_MODEL
OFF_XBC = OFF_MSSM + D_MODEL
OFF_G = OFF_XBC + CONV_DIM
OFF_DT = OFF_G + V_W
REST_W = OFF_DT + DT_W

VMEM_LIMIT = 56 * 1024 * 1024


def _cparams(sem):
    return pltpu.CompilerParams(dimension_semantics=sem, vmem_limit_bytes=VMEM_LIMIT)


def _inproj_kernel(x_ref, g_ref, w_ref, o_ref, u_ref):
    @pl.when(pl.program_id(1) == 0)
    def _():
        x = x_ref[...]
        ms = jnp.mean(x * x, axis=-1, keepdims=True)
        u_ref[...] = (x * lax.rsqrt(ms + EPS) * g_ref[...]).astype(BF16)

    o_ref[...] = lax.dot_general(u_ref[...], w_ref[...], _NT, preferred_element_type=F32).astype(o_ref.dtype)


def _inproj(x, g, w, out_dtype, tm, tn):
    m, d = x.shape
    n = w.shape[0]
    tm = min(tm, m)
    return pl.pallas_call(
        _inproj_kernel,
        grid=(m // tm, n // tn),
        in_specs=[
            pl.BlockSpec((tm, d), lambda i, j: (i, 0)),
            pl.BlockSpec((1, d), lambda i, j: (0, 0)),
            pl.BlockSpec((tn, d), lambda i, j: (j, 0)),
        ],
        out_specs=pl.BlockSpec((tm, tn), lambda i, j: (i, j)),
        out_shape=jax.ShapeDtypeStruct((m, n), out_dtype),
        scratch_shapes=[pltpu.VMEM((tm, d), BF16)],
        compiler_params=_cparams(("parallel", "arbitrary")),
        name="in_proj",
    )(x, g, w)


ONES_ROWS = 16
LOG2E = math.log2(math.e)
_NT = (((1,), (1,)), ((), ()))


def _attn_kernel(slope_ref, lq1_ref, lk1_ref, lq2_ref, lk2_ref, sg_ref,
                 q_ref, k_ref, v_ref, km_ref, vm_ref, g_ref, o_ref,
                 vt_ref, vmt_ref, kpos_ref, qz_ref, sa_ref, sb_ref, cmax_a, cmax_b, m_ref, acc_ref,
                 *, tq, tk):
    h = pl.program_id(1)
    qi = pl.program_id(2)
    slope = slope_ref[h, 0]
    nchunk = k_ref.shape[0] // tk
    r_e = lax.broadcasted_iota(jnp.int32, (128, 128), 0)
    c_e = lax.broadcasted_iota(jnp.int32, (128, 128), 1)
    eye = (r_e == c_e).astype(BF16)

    @pl.when(qi == 0)
    def _():
        ones = jnp.ones((ONES_ROWS, tk), BF16)

        def tr(j, carry):
            off = pl.multiple_of(j * tk, tk)
            vt = lax.dot_general(eye, v_ref[pl.ds(off, tk), :], _NT, preferred_element_type=F32)
            vt_ref[j, 0:A_DV, :] = vt.astype(BF16)
            vt_ref[j, A_DV:, :] = ones
            return carry
        lax.fori_loop(0, nchunk, tr, 0, unroll=4)
        vmt = lax.dot_general(eye, vm_ref[...], _NT, preferred_element_type=F32)
        vmt_ref[0:A_DV, :] = vmt.astype(BF16)
        vmt_ref[A_DV:, :] = ones[:, :PREFIX]

        row = lax.broadcasted_iota(jnp.int32, (tk, 128), 0)
        lane = lax.broadcasted_iota(jnp.int32, (tk, 128), 1)
        lo = row % 256
        kpos = jnp.where(lane < 3, lo, jnp.where(lane < 6, row - lo, 0))
        kpos_ref[...] = kpos.astype(F32).astype(BF16)
        arow = lax.broadcasted_iota(jnp.int32, (128, 2 * tq), 0)
        piece = jnp.where(arow % 3 == 0, slope_ref[h, 1], jnp.where(arow % 3 == 1, slope_ref[h, 2], slope_ref[h, 3]))
        qz_ref[128:256, :] = jnp.where(arow < 6, piece, 0.0).astype(BF16)

    qt = lax.dot_general(eye, q_ref[...], _NT, preferred_element_type=F32).astype(BF16)
    drow = lax.broadcasted_iota(jnp.int32, qt.shape, 0)
    zero = jnp.zeros_like(qt)
    qz_ref[0:128, :tq] = jnp.where(drow < A_DK, qt, zero)
    qz_ref[0:128, tq:] = jnp.where(drow >= A_DK, qt, zero)

    m_ref[...] = jnp.full(m_ref.shape, NEG, F32)
    acc_ref[...] = jnp.zeros(acc_ref.shape, F32)

    nstrip = 2 * tq // tk
    strip = lambda i: slice(i * tk, (i + 1) * tk)

    def scores(k, nk, i):
        lhs = jnp.concatenate([k, kpos_ref[0:nk, :]], axis=1)
        return jnp.dot(lhs, qz_ref[:, strip(i)], preferred_element_type=F32)

    def qk(kc, i, s_buf, cmax_ref):
        off = pl.multiple_of(kc * tk, tk)
        s = scores(k_ref[pl.ds(off, tk), :], tk, i)
        s_buf[:, strip(i)] = s
        cmax_ref[:, strip(i)] = jnp.max(s, axis=0, keepdims=True)

    def update(i, s, cmax, vt, c0):
        m_old = m_ref[:, strip(i)]
        m_new = jnp.maximum(m_old, cmax + c0)
        p = jnp.exp2(s - (m_new - c0)).astype(BF16)
        alpha = jnp.exp2(m_old - m_new)
        acc_ref[:, strip(i)] = alpha * acc_ref[:, strip(i)] + jnp.dot(vt, p, preferred_element_type=F32)
        m_ref[:, strip(i)] = m_new

    def masked_update(i, s, keep, vt, c0):
        s = jnp.where(keep, s, NEG)
        update(i, s, jnp.max(s, axis=0, keepdims=True), vt, c0)

    def chunk_offset(kc):
        return -slope * (qi * tq - kc * tk).astype(F32)

    s_prefix = [scores(km_ref[...], PREFIX, i) for i in range(nstrip)]
    for i in range(nstrip):
        qk(0, i, sa_ref, cmax_a)
    krow = lax.broadcasted_iota(jnp.int32, (PREFIX, tk), 0)
    for i in range(nstrip):
        masked_update(i, s_prefix[i], krow >= N_PAD, vmt_ref[...], -slope * ((qi * tq).astype(F32) + PREFIX))

    def pair(j):
        kc = 2 * j
        for i in range(nstrip):
            qk(kc + 1, i, sb_ref, cmax_b)
            update(i, sa_ref[:, strip(i)], cmax_a[:, strip(i)], vt_ref[kc], chunk_offset(kc))
        for i in range(nstrip):
            qk(kc + 2, i, sa_ref, cmax_a)
            update(i, sb_ref[:, strip(i)], cmax_b[:, strip(i)], vt_ref[kc + 1], chunk_offset(kc + 1))

    def two_pairs(t, carry):
        pair(2 * t)
        pair(2 * t + 1)
        return carry

    lax.fori_loop(0, qi // 2, two_pairs, 0)

    @pl.when(qi % 2 == 1)
    def _():
        pair(qi - 1)

    kk = lax.broadcasted_iota(jnp.int32, (tk, tk), 0)
    qq = lax.broadcasted_iota(jnp.int32, (tk, tk), 1)
    causal = kk <= qq
    kd = 2 * qi
    for i in range(nstrip):
        if i % 2 == 0:
            masked_update(i, sa_ref[:, strip(i)], causal, vt_ref[kd], chunk_offset(kd))
        else:
            qk(kd + 1, i, sb_ref, cmax_b)
            update(i, sa_ref[:, strip(i)], cmax_a[:, strip(i)], vt_ref[kd], chunk_offset(kd))
    for i in range(1, nstrip, 2):
        masked_update(i, sb_ref[:, strip(i)], causal, vt_ref[kd + 1], chunk_offset(kd + 1))

    lam =(jnp.exp(jnp.sum(lq1_ref[...] * lk1_ref[...], axis=-1, keepdims=True))
           - jnp.exp(jnp.sum(lq2_ref[...] * lk2_ref[...], axis=-1, keepdims=True)) + LAM_INIT)
    acc = acc_ref[...]
    o = acc[0:A_DV, :] / acc[A_DV:A_DV + 1, :]
    oa = (o[:, :tq] - lam * o[:, tq:]).T
    ms = jnp.mean(oa * oa, axis=-1, keepdims=True)
    oa = oa * lax.rsqrt(ms + EPS) * sg_ref[...] * (1.0 - LAM_INIT)
    g = g_ref[...]
    o_ref[...] = (oa * (g * jax.nn.sigmoid(g))).astype(o_ref.dtype)


def _slope_table():
    slope2 = (LOG2E * 2.0 ** (-8.0 * np.arange(1, A_HEADS + 1) / A_HEADS)).astype(np.float32)

    def top16(v):
        return (v.astype(np.float32).view(np.uint32) & np.uint32(0xFFFF0000)).view(np.float32)

    b1 = top16(slope2)
    b2 = top16(slope2 - b1)
    b3 = top16(slope2 - b1 - b2)
    return np.stack([slope2, b1, b2, b3], axis=1)


def _attention(slopes, lq1, lk1, lq2, lk2, subln_g, qkv, kv0, rest, tq):
    b, s, _ = qkv.shape
    assert s % tq == 0 and tq % 2 == 0
    tk = tq // 2
    nh = A_HEADS
    small = lambda w: pl.BlockSpec((1, w), lambda bi, hi, qi: (0, 0))
    return pl.pallas_call(
        functools.partial(_attn_kernel, tq=tq, tk=tk),
        grid=(b, nh, s // tq),
        in_specs=[
            pl.BlockSpec(memory_space=pltpu.SMEM),
            small(A_DK), small(A_DK), small(A_DK), small(A_DK), small(A_DV),
            pl.BlockSpec((None, tq, 128), lambda bi, hi, qi: (bi, qi, hi)),
            pl.BlockSpec((None, s, 128), lambda bi, hi, qi: (bi, 0, nh + hi)),
            pl.BlockSpec((None, s, 128), lambda bi, hi, qi: (bi, 0, 2 * nh + hi)),
            pl.BlockSpec((PREFIX, 128), lambda bi, hi, qi: (0, hi)),
            pl.BlockSpec((PREFIX, 128), lambda bi, hi, qi: (0, nh + hi)),
            pl.BlockSpec((None, tq, 128), lambda bi, hi, qi: (bi, qi, OFF_G // 128 + hi)),
        ],
        out_specs=pl.BlockSpec((None, tq, 128), lambda bi, hi, qi: (bi, qi, hi)),
        out_shape=jax.ShapeDtypeStruct((b, s, V_W), BF16),
        scratch_shapes=[pltpu.VMEM((s // tk, A_DV + ONES_ROWS, tk), BF16),
                        pltpu.VMEM((A_DV + ONES_ROWS, PREFIX), BF16),
                        pltpu.VMEM((tk, 128), BF16),
                        pltpu.VMEM((256, 2 * tq), BF16),
                        pltpu.VMEM((tk, 2 * tq), F32),
                        pltpu.VMEM((tk, 2 * tq), F32),
                        pltpu.VMEM((1, 2 * tq), F32),
                        pltpu.VMEM((1, 2 * tq), F32),
                        pltpu.VMEM((1, 2 * tq), F32),
                        pltpu.VMEM((A_DV + ONES_ROWS, 2 * tq), F32)],
        compiler_params=_cparams(("parallel", "parallel", "arbitrary")),
        name="diff_attention",
    )(slopes, lq1, lk1, lq2, lk2, subln_g, qkv, qkv, qkv, kv0, kv0, rest)


def _sigmoid(x):
    return 0.5 * jnp.tanh(0.5 * x) + 0.5


def _silu(x):
    return x * _sigmoid(x)


def _split3(x):
    def top16(v):
        bits = lax.bitcast_convert_type(v, jnp.uint32) & jnp.uint32(0xFFFF0000)
        return lax.bitcast_convert_type(bits, F32)
    p1 = top16(x)
    r1 = x - p1
    p2 = top16(r1)
    return p1.astype(BF16), p2.astype(BF16), (r1 - p2).astype(BF16)


def _ssd_kernel(z_ref, xbc_ref, dt_ref, cw_ref, cb_ref, dtb_ref, alog_ref, dskip_ref, ng_ref,
                exp_ref, st0_ref, tail0_ref,
                y_ref, stout_ref, tailout_ref,
                hist_ref, state_ref, *, q, n_pad):
    c = pl.program_id(1)

    @pl.when(c == 0)
    def _():
        hist_ref[0:8, :] = tail0_ref[...]
        state_ref[...] = st0_ref[...]

    hist_ref[8:8 + q, :] = xbc_ref[...]
    xpad = hist_ref[...]
    conv = cb_ref[...] + cw_ref[S_CONV - 1:S_CONV, :] * xpad[8:, :]
    for back in range(1, S_CONV):
        conv = conv + cw_ref[S_CONV - 1 - back:S_CONV - back, :] * pltpu.roll(xpad, back, 0)[8:, :]
    hist_ref[0:8, :] = hist_ref[q:q + 8, :]
    xbc = _silu(conv)

    dt_raw = dt_ref[:, :S_HEADS] + dtb_ref[...]
    dt = jnp.maximum(dt_raw, 0.0) + jnp.log(1.0 + jnp.exp(-jnp.abs(dt_raw)))
    if n_pad:
        row = lax.broadcasted_iota(jnp.int32, (q, 1), 0)
        valid = row >= n_pad
        xbc = jnp.where(valid, xbc, 0.0)
        dt = jnp.where(valid, dt, 0.0)
    a = -jnp.exp(alog_ref[...])
    da = dt * a

    r_i = lax.broadcasted_iota(jnp.int32, (q, q), 0)
    c_i = lax.broadcasted_iota(jnp.int32, (q, q), 1)
    tril = c_i <= r_i
    acs = jnp.dot(tril.astype(F32), da, preferred_element_type=F32, precision=lax.Precision.HIGHEST)
    acs_t = acs.T

    pieces = jnp.concatenate(_split3(jnp.concatenate([dt, acs], axis=0)), axis=1)
    bm = xbc[:, D_SSM:D_SSM + BC_W].astype(BF16)
    cm = xbc[:, D_SSM + BC_W:].astype(BF16)
    lane = lax.broadcasted_iota(jnp.int32, (q, 128), 1)
    hpg = S_HEADS // S_GROUPS

    for g in range(S_GROUPS):
        gs = slice(g * GROUP_W, (g + 1) * GROUP_W)
        wide = jnp.dot(pieces, exp_ref[:, gs], preferred_element_type=F32)
        dt_w = wide[:q]
        acs_w = wide[q:]
        acs_last = acs_w[q - 1:q, :]
        xs = xbc[:, gs]
        xdt = xs * dt_w
        xdec_b = (xdt * jnp.exp(acs_last - acs_w)).astype(BF16)
        state_old = state_ref[:, gs]
        bg = bm[:, g * S_STATE:(g + 1) * S_STATE]
        cg = cm[:, g * S_STATE:(g + 1) * S_STATE]
        cbm = lax.dot_general(cg, bg, (((1,), (1,)), ((), ())), preferred_element_type=F32)
        y_off = jnp.dot(cg, state_old.astype(BF16), preferred_element_type=F32)
        state_ref[:, gs] = jnp.exp(acs_last) * state_old + lax.dot_general(
            bg, xdec_b, (((0,), (0,)), ((), ())), preferred_element_type=F32)
        pairs = []
        for pair in range(hpg // 2):
            h0 = g * hpg + 2 * pair
            xp = xdt[:, 2 * pair * S_HEADDIM:(2 * pair + 2) * S_HEADDIM].astype(BF16)
            ys = []
            for hh in (h0, h0 + 1):
                seg = acs[:, hh:hh + 1] - acs_t[hh:hh + 1, :]
                lmat = jnp.exp(jnp.where(tril, seg, -jnp.inf))
                ys.append(jnp.dot((cbm * lmat).astype(BF16), xp, preferred_element_type=F32))
            pairs.append(jnp.where(lane < S_HEADDIM, ys[0], ys[1]))
        y = jnp.concatenate(pairs, axis=1) + y_off * jnp.exp(acs_w)
        y = y + xs * dskip_ref[:, gs]
        yg = y * _silu(z_ref[:, gs])
        ms = jnp.mean(yg * yg, axis=-1, keepdims=True)
        y_ref[:, gs] = (yg * lax.rsqrt(ms + EPS) * ng_ref[:, gs]).astype(y_ref.dtype)

    @pl.when(c == pl.num_programs(1) - 1)
    def _():
        stout_ref[...] = state_ref[...]
        tailout_ref[...] = hist_ref[0:8, :]


def _ssd(rest, conv_w, conv_b, dt_bias, a_log, dskip_w, norm_g, expand, state0, tail0, q, n_pad):
    b, s, _ = rest.shape
    q = min(q, s)
    const = lambda shp: pl.BlockSpec(shp, lambda bi, ci: (0,) * len(shp))
    return pl.pallas_call(
        functools.partial(_ssd_kernel, q=q, n_pad=n_pad),
        grid=(b, s // q),
        in_specs=[
            pl.BlockSpec((None, q, D_SSM), lambda bi, ci: (bi, ci, OFF_Z // D_SSM)),
            pl.BlockSpec((None, q, CONV_DIM), lambda bi, ci: (bi, ci, OFF_XBC // CONV_DIM)),
            pl.BlockSpec((None, q, 128), lambda bi, ci: (bi, ci, OFF_DT // 128)),
            const((S_CONV, CONV_DIM)), const((1, CONV_DIM)), const((1, S_HEADS)), const((1, S_HEADS)),
            const((1, D_SSM)), const((1, D_SSM)), const((3 * S_HEADS, D_SSM)),
            const((S_STATE, D_SSM)), const((8, CONV_DIM)),
        ],
        out_specs=[
            pl.BlockSpec((None, q, D_SSM), lambda bi, ci: (bi, ci, 0)),
            pl.BlockSpec((None, S_STATE, D_SSM), lambda bi, ci: (bi, 0, 0)),
            pl.BlockSpec((None, 8, CONV_DIM), lambda bi, ci: (bi, 0, 0)),
        ],
        out_shape=[
            jax.ShapeDtypeStruct((b, s, D_SSM), BF16),
            jax.ShapeDtypeStruct((b, S_STATE, D_SSM), F32),
            jax.ShapeDtypeStruct((b, 8, CONV_DIM), F32),
        ],
        scratch_shapes=[pltpu.VMEM((8 + q, CONV_DIM), F32), pltpu.VMEM((S_STATE, D_SSM), F32)],
        compiler_params=_cparams(("parallel", "arbitrary")),
        name="ssd",
    )(rest, rest, rest, conv_w, conv_b, dt_bias, a_log, dskip_w, norm_g, expand, state0, tail0)


def _out_kernel(x_ref, oa_ref, yg_ref, ma_ref, ms_ref, wa_ref, ws_ref, wo_ref, fg_ref, o_ref):
    y_att = jnp.dot(oa_ref[...], wa_ref[...], preferred_element_type=F32)
    y_ssm = jnp.dot(yg_ref[...], ws_ref[...], preferred_element_type=F32)
    merged = jax.nn.sigmoid(ma_ref[...]) * y_att + jax.nn.sigmoid(ms_ref[...]) * y_ssm
    hres = x_ref[...] + jnp.dot(merged.astype(BF16), wo_ref[...], preferred_element_type=F32)
    ms = jnp.mean(hres * hres, axis=-1, keepdims=True)
    o_ref[...] = hres * lax.rsqrt(ms + EPS) * fg_ref[...]


def _out(x, oa, yg, rest, w_a, w_s, w_o, final_g, tm):
    m, d = x.shape
    tm = min(tm, m)
    once = pl.Buffered(1)
    return pl.pallas_call(
        _out_kernel,
        grid=(m // tm,),
        in_specs=[
            pl.BlockSpec((tm, d), lambda i: (i, 0)),
            pl.BlockSpec((tm, V_W), lambda i: (i, 0)),
            pl.BlockSpec((tm, D_SSM), lambda i: (i, 0)),
            pl.BlockSpec((tm, d), lambda i: (i, OFF_MATT // D_MODEL)),
            pl.BlockSpec((tm, d), lambda i: (i, OFF_MSSM // D_MODEL)),
            pl.BlockSpec((V_W, d), lambda i: (0, 0), pipeline_mode=once),
            pl.BlockSpec((D_SSM, d), lambda i: (0, 0), pipeline_mode=once),
            pl.BlockSpec((d, d), lambda i: (0, 0), pipeline_mode=once),
            pl.BlockSpec((1, d), lambda i: (0, 0)),
        ],
        out_specs=pl.BlockSpec((tm, d), lambda i: (i, 0)),
        out_shape=jax.ShapeDtypeStruct((m, d), F32),
        compiler_params=_cparams(("parallel",)),
        name="out_proj",
    )(x, oa, yg, rest, rest, w_a, w_s, w_o, final_g)


def kernel(x, meta, norm_g, w_in, conv_w, conv_b, dt_bias, a_log, d_skip, ssm_norm_g,
           lam_q1, lam_k1, lam_q2, lam_k2, subln_g, w_br_attn, w_br_ssm, w_out, final_g):
    b, s, d = x.shape
    assert d == D_MODEL and norm_g.shape[0] == 1 and s % 128 == 0

    w = jnp.swapaxes(w_in[0], 0, 1)
    o_q, o_k, o_v, o_g = 0, Q_W, 2 * Q_W, 2 * Q_W + V_W
    o_z = o_g + V_W
    o_xbc = o_z + D_SSM
    o_dt = o_xbc + CONV_DIM
    o_ma = o_dt + S_HEADS
    o_ms = o_ma + D_MODEL
    w_qkv = jnp.concatenate([w[o_q:o_k] * (A_DK ** -0.5 * LOG2E), w[o_k:o_g]], axis=0).astype(BF16)
    w_rest = jnp.concatenate([
        w[o_z:o_xbc], w[o_ma:o_ms], w[o_ms:o_ms + D_MODEL], w[o_xbc:o_dt], w[o_g:o_z],
        w[o_dt:o_ma], jnp.zeros((DT_W - S_HEADS, d), w.dtype)], axis=0).astype(BF16)
    w_kv = w_qkv[Q_W:]

    g_in = norm_g[0][None, :]
    xf = x.reshape(b * s, d)
    h0 = jnp.concatenate([jnp.zeros((N_PAD, d), x.dtype), meta.astype(x.dtype)], axis=0)

    qkv = _inproj(xf, g_in, w_qkv, BF16, 1024, 1536).reshape(b, s, 3 * Q_W)
    rest = _inproj(xf, g_in, w_rest, F32, 1024, 1536).reshape(b, s, REST_W)
    kv0 = _inproj(h0, g_in, w_kv, BF16, PREFIX, 1024)
    rest0 = _inproj(h0, g_in, w_rest, F32, PREFIX, 1536).reshape(1, PREFIX, REST_W)

    slopes = jnp.asarray(_slope_table())
    row = lambda v: v.astype(F32).reshape(1, -1)
    oa = _attention(slopes, row(lam_q1[0]), row(lam_k1[0]), row(lam_q2[0]), row(lam_k2[0]), row(subln_g[0]),
                    qkv, kv0, rest, min(1024, s))

    dskip_w = jnp.repeat(d_skip[0].astype(F32), S_HEADDIM)[None, :]
    expand = jnp.tile(jnp.repeat(jnp.eye(S_HEADS, dtype=BF16), S_HEADDIM, axis=1), (3, 1))
    ssd_args = (conv_w[0].astype(F32), row(conv_b[0]), row(dt_bias[0]), row(a_log[0]), dskip_w,
                row(ssm_norm_g[0]), expand)
    zero_state = jnp.zeros((S_STATE, D_SSM), F32)
    zero_tail = jnp.zeros((8, CONV_DIM), F32)
    _, state0, tail0 = _ssd(rest0, *ssd_args, zero_state, zero_tail, PREFIX, N_PAD)
    yg, _, _ = _ssd(rest, *ssd_args, state0[0], tail0[0], 128, 0)

    out = _out(xf, oa.reshape(b * s, V_W), yg.reshape(b * s, D_SSM), rest.reshape(b * s, REST_W),
               w_br_attn[0].astype(BF16), w_br_ssm[0].astype(BF16), w_out[0].astype(BF16),
               row(final_g), 256)
    return out.reshape(b, s, d)
```
